```python
import math
import jax, jax.numpy as jnp
from jax import lax
import numpy as np

D_MODEL = 2048
BATCH = 4
SEQ = 8192
DEPTH = 1

N_META = 16
LN_EPS = 1e-5
ALPHA = (2 * DEPTH) ** 0.25
BETA = (8 * DEPTH) ** -0.25
POOL_WINDOWS = (2, 4, 8, 16)
N_POOL_GROUPS = len(POOL_WINDOWS)
POOL_GROUP_DIM = D_MODEL // 8
POOL_WIDTH = N_POOL_GROUPS * POOL_GROUP_DIM
SSD_HEAD_DIM = 64
SSD_WIDTH = 3 * D_MODEL // 2
SSD_HEADS = SSD_WIDTH // SSD_HEAD_DIM
SSD_GROUPS = 8
SSD_STATE = 128
CONV_K = 4
CHUNK = 128
DT_MIN = 0.001
DT_MAX = 0.1
CONV_DIM = SSD_WIDTH + 2 * SSD_GROUPS * SSD_STATE
MIX_WIDTH = POOL_WIDTH + SSD_WIDTH
IN_WIDTH = POOL_WIDTH + SSD_WIDTH + CONV_DIM + SSD_HEADS
PEER_HEADS = 8
PEER_NKEYS = 128
PEER_EXPERTS = PEER_NKEYS * PEER_NKEYS
PEER_TOPK = 16
PEER_QDIM = 256
PEER_BLOCK = 256

kernel_name = "hymba_pool_ssd_peer_deepnorm"


def layer_norm(x, g, b):
    x32 = x.astype(jnp.float32)
    mu = jnp.mean(x32, axis=-1, keepdims=True)
    var = jnp.mean(jnp.square(x32 - mu), axis=-1, keepdims=True)
    y = (x32 - mu) * lax.rsqrt(var + LN_EPS) * g.astype(jnp.float32) + b.astype(jnp.float32)
    return y.astype(x.dtype)


def pool_mix(p, w_pool, scale):
    bsz, L, _ = p.shape
    pg = p.astype(jnp.float32).reshape(bsz, L, N_POOL_GROUPS, POOL_GROUP_DIM)
    cs = jnp.concatenate([jnp.zeros_like(pg[:, :1]), jnp.cumsum(pg, axis=1)], axis=1)
    t = jnp.arange(L)
    outs = []
    for gi, w in enumerate(POOL_WINDOWS):
        start = jnp.maximum(t + 1 - w, 0)
        count = (t + 1 - start).astype(jnp.float32)
        win_sum = cs[:, 1:, gi] - cs[:, start, gi]
        outs.append(win_sum / count[None, :, None] - pg[:, :, gi])
    pooled = jnp.stack(outs, axis=2)
    mixed = jnp.einsum('blgc,gcd->blgd', pooled, w_pool.astype(jnp.float32))
    mixed = mixed.reshape(bsz, L, POOL_WIDTH) * scale.astype(jnp.float32)
    return mixed.astype(p.dtype)


def causal_dwconv(u, w, b):
    out = lax.conv_general_dilated(u, w[:, None, :].astype(u.dtype), window_strides=(1,),
                                   padding=[(CONV_K - 1, 0)],
                                   dimension_numbers=('NWC', 'WIO', 'NWC'),
                                   feature_group_count=u.shape[-1])
    return out + b.astype(u.dtype)


def ssd_chunked(x, dt, A, Bm, Cm):
    bsz, L, H, P = x.shape
    G, N = Bm.shape[2], Bm.shape[3]
    R = H // G
    n_pad = (-L) % CHUNK
    pad4 = ((0, 0), (n_pad, 0), (0, 0), (0, 0))
    x = jnp.pad(x, pad4)
    Bm = jnp.pad(Bm, pad4)
    Cm = jnp.pad(Cm, pad4)
    dt = jnp.pad(dt, ((0, 0), (n_pad, 0), (0, 0)))
    Lp = L + n_pad
    nc = Lp // CHUNK
    X = (x * dt[..., None]).reshape(bsz, nc, CHUNK, G, R, P)
    Adt = (dt * A).reshape(bsz, nc, CHUNK, G, R)
    Bc = Bm.reshape(bsz, nc, CHUNK, G, N)
    Cc = Cm.reshape(bsz, nc, CHUNK, G, N)
    xs = tuple(jnp.moveaxis(a, 1, 0) for a in (X, Adt, Bc, Cc))
    causal = jnp.tril(jnp.ones((CHUNK, CHUNK), dtype=bool))[None, :, :, None, None]

    def step(state, inp):
        Xk, Ak, Bk, Ck = inp
        Acs = jnp.cumsum(Ak, axis=1)
        seg = Acs[:, :, None] - Acs[:, None, :]
        Lmat = jnp.exp(jnp.where(causal, seg, -jnp.inf))
        CB = jnp.einsum('blgn,bsgn->blsg', Ck, Bk)
        y_diag = jnp.einsum('blsg,blsgr,bsgrp->blgrp', CB, Lmat, Xk)
        y_off = jnp.einsum('blgn,bgrpn,blgr->blgrp', Ck, state, jnp.exp(Acs))
        decay_end = jnp.exp(Acs[:, -1:] - Acs)
        new_state = state * jnp.exp(Acs[:, -1])[..., None, None] + \
            jnp.einsum('bsgn,bsgr,bsgrp->bgrpn', Bk, decay_end, Xk)
        return new_state, y_diag + y_off

    state0 = jnp.zeros((bsz, G, R, P, N), jnp.float32)
    _, ys = lax.scan(step, state0, xs)
    y = jnp.moveaxis(ys, 0, 1).reshape(bsz, Lp, H, P)
    return y[:, n_pad:]


def ssd_mix(z, xbc, dt_raw, conv_w, conv_b, dt_bias, A_log, D_skip, norm_g):
    bsz, L, _ = z.shape
    xbc = jax.nn.silu(causal_dwconv(xbc, conv_w, conv_b)).astype(jnp.float32)
    xs = xbc[..., :SSD_WIDTH].reshape(bsz, L, SSD_HEADS, SSD_HEAD_DIM)
    Bm = xbc[..., SSD_WIDTH:SSD_WIDTH + SSD_GROUPS * SSD_STATE].reshape(bsz, L, SSD_GROUPS, SSD_STATE)
    Cm = xbc[..., SSD_WIDTH + SSD_GROUPS * SSD_STATE:].reshape(bsz, L, SSD_GROUPS, SSD_STATE)
    dt = jax.nn.softplus(dt_raw.astype(jnp.float32) + dt_bias.astype(jnp.float32))
    A = -jnp.exp(A_log.astype(jnp.float32))
    y = ssd_chunked(xs, dt, A, Bm, Cm)
    y = y + D_skip.astype(jnp.float32)[:, None] * xs
    y = y.reshape(bsz, L, SSD_WIDTH) * jax.nn.silu(z.astype(jnp.float32))
    yg = y.reshape(bsz, L, SSD_GROUPS, SSD_WIDTH // SSD_GROUPS)
    yg = yg * lax.rsqrt(jnp.mean(jnp.square(yg), axis=-1, keepdims=True) + LN_EPS)
    y = yg.reshape(bsz, L, SSD_WIDTH) * norm_g.astype(jnp.float32)
    return y.astype(z.dtype)


def peer_ffn(h, wq, sub_keys, u_tab, v_tab):
    bsz, L, D = h.shape
    T = bsz * L
    n_blk = -(-T // PEER_BLOCK)
    toks = jnp.pad(h.reshape(T, D), ((0, n_blk * PEER_BLOCK - T), (0, 0)))
    toks = toks.reshape(n_blk, PEER_BLOCK, D)

    def block(xb):
        q = (xb @ wq).reshape(PEER_BLOCK, PEER_HEADS, 2, PEER_QDIM // 2)
        s = jnp.einsum('thic,ikc->thik', q, sub_keys).astype(jnp.float32)
        s_top, i_top = lax.top_k(s, PEER_TOPK)
        cand = s_top[:, :, 0, :, None] + s_top[:, :, 1, None, :]
        c_top, c_idx = lax.top_k(cand.reshape(PEER_BLOCK, PEER_HEADS, PEER_TOPK * PEER_TOPK), PEER_TOPK)
        i1 = jnp.take_along_axis(i_top[:, :, 0], c_idx // PEER_TOPK, axis=-1)
        i2 = jnp.take_along_axis(i_top[:, :, 1], c_idx % PEER_TOPK, axis=-1)
        expert = (i1 * PEER_NKEYS + i2).reshape(PEER_BLOCK, PEER_HEADS * PEER_TOPK)
        gate = jax.nn.softmax(c_top, axis=-1).reshape(PEER_BLOCK, PEER_HEADS * PEER_TOPK)
        u = u_tab[expert]
        act = jax.nn.gelu(jnp.einsum('td,tkd->tk', xb, u).astype(jnp.float32), approximate=False)
        a = (gate * act).astype(xb.dtype)
        return jnp.einsum('tk,tkd->td', a, v_tab[expert])

    out = lax.map(block, toks).reshape(n_blk * PEER_BLOCK, D)[:T]
    return out.reshape(bsz, L, D).astype(h.dtype)


def hybrid_layer(h, w_in, pool_w, pool_scale, conv_w, conv_b, dt_bias, A_log, D_skip,
                 ssd_norm_g, w_out, ln1_g, ln1_b, peer_wq, peer_keys, peer_u, peer_v, ln2_g, ln2_b):
    proj = h @ w_in
    o1 = POOL_WIDTH
    o2 = o1 + SSD_WIDTH
    o3 = o2 + CONV_DIM
    p_in = proj[..., :o1]
    z = proj[..., o1:o2]
    xbc = proj[..., o2:o3]
    dt_raw = proj[..., o3:]
    pool_out = pool_mix(p_in, pool_w, pool_scale)
    ssd_out = ssd_mix(z, xbc, dt_raw, conv_w, conv_b, dt_bias, A_log, D_skip, ssd_norm_g)
    mix = jnp.concatenate([pool_out, ssd_out.astype(pool_out.dtype)], axis=-1) @ w_out
    h = layer_norm(ALPHA * h + mix.astype(h.dtype), ln1_g, ln1_b)
    h = layer_norm(ALPHA * h + peer_ffn(h, peer_wq, peer_keys, peer_u, peer_v), ln2_g, ln2_b)
    return h


def setup_inputs(seed: int = 0) -> dict:
    key = jax.random.key(seed)
    ks = jax.random.split(key, 24)
    f32 = jnp.float32

    def nrm(k, shape, s):
        return jax.random.normal(k, shape, f32) * s

    x = nrm(ks[0], (BATCH, SEQ, D_MODEL), 1.0)
    meta_tokens = nrm(ks[1], (N_META, D_MODEL), 1.0)
    ln_in_g = 1.0 + nrm(ks[2], (D_MODEL,), 0.02)
    ln_in_b = nrm(ks[3], (D_MODEL,), 0.02)
    w_in = nrm(ks[4], (DEPTH, D_MODEL, IN_WIDTH), D_MODEL ** -0.5)
    pool_w = nrm(ks[5], (DEPTH, N_POOL_GROUPS, POOL_GROUP_DIM, POOL_GROUP_DIM), POOL_GROUP_DIM ** -0.5)
    pool_scale = 1.0 + nrm(ks[6], (DEPTH, POOL_WIDTH), 0.02)
    conv_w = nrm(ks[7], (DEPTH, CONV_K, CONV_DIM), CONV_K ** -0.5)
    conv_b = nrm(ks[8], (DEPTH, CONV_DIM), 0.01)
    u = jax.random.uniform(ks[9], (DEPTH, SSD_HEADS), f32)
    dt0 = jnp.exp(u * (math.log(DT_MAX) - math.log(DT_MIN)) + math.log(DT_MIN))
    dt_bias = dt0 + jnp.log(-jnp.expm1(-dt0))
    A_log = jnp.log(jax.random.uniform(ks[10], (DEPTH, SSD_HEADS), f32, minval=1.0, maxval=16.0))
    D_skip = 1.0 + nrm(ks[11], (DEPTH, SSD_HEADS), 0.01)
    ssd_norm_g = 1.0 + nrm(ks[12], (DEPTH, SSD_WIDTH), 0.02)
    w_out = nrm(ks[13], (DEPTH, MIX_WIDTH, D_MODEL), BETA * MIX_WIDTH ** -0.5)
    ln1_g = 1.0 + nrm(ks[14], (DEPTH, D_MODEL), 0.02)
    ln1_b = nrm(ks[15], (DEPTH, D_MODEL), 0.02)
    peer_wq = nrm(ks[16], (DEPTH, D_MODEL, PEER_HEADS * PEER_QDIM), D_MODEL ** -0.5)
    peer_keys = nrm(ks[17], (DEPTH, 2, PEER_NKEYS, PEER_QDIM // 2), (PEER_QDIM // 2) ** -0.5)
    peer_u = nrm(ks[18], (DEPTH, PEER_EXPERTS, D_MODEL), D_MODEL ** -0.5)
    peer_v = nrm(ks[19], (DEPTH, PEER_EXPERTS, D_MODEL), BETA * PEER_HEADS ** -0.5)
    ln2_g = 1.0 + nrm(ks[20], (DEPTH, D_MODEL), 0.02)
    ln2_b = nrm(ks[21], (DEPTH, D_MODEL), 0.02)
    return {"x": x, "meta_tokens": meta_tokens, "ln_in_g": ln_in_g, "ln_in_b": ln_in_b,
            "w_in": w_in, "pool_w": pool_w, "pool_scale": pool_scale, "conv_w": conv_w,
            "conv_b": conv_b, "dt_bias": dt_bias, "A_log": A_log, "D_skip": D_skip,
            "ssd_norm_g": ssd_norm_g, "w_out": w_out, "ln1_g": ln1_g, "ln1_b": ln1_b,
            "peer_wq": peer_wq, "peer_keys": peer_keys, "peer_u": peer_u, "peer_v": peer_v,
            "ln2_g": ln2_g, "ln2_b": ln2_b}


def reference(x, meta_tokens, ln_in_g, ln_in_b, w_in, pool_w, pool_scale, conv_w, conv_b,
              dt_bias, A_log, D_skip, ssd_norm_g, w_out, ln1_g, ln1_b, peer_wq, peer_keys,
              peer_u, peer_v, ln2_g, ln2_b):
    bsz = x.shape[0]
    meta = jnp.broadcast_to(meta_tokens[None].astype(x.dtype), (bsz, N_META, x.shape[-1]))
    h = jnp.concatenate([meta, x], axis=1)
    h = layer_norm(h, ln_in_g, ln_in_b)
    for l in range(DEPTH):
        h = hybrid_layer(h, w_in[l], pool_w[l], pool_scale[l], conv_w[l], conv_b[l], dt_bias[l],
                         A_log[l], D_skip[l], ssd_norm_g[l], w_out[l], ln1_g[l], ln1_b[l],
                         peer_wq[l], peer_keys[l], peer_u[l], peer_v[l], ln2_g[l], ln2_b[l])
    return h[:, N_META:]
```

```python
import functools
import math

import numpy as np
import jax
import jax.numpy as jnp
from jax import lax
from jax.experimental import pallas as pl
from jax.experimental.pallas import tpu as pltpu

F32 = jnp.float32
BF16 = jnp.bfloat16

N_META = 16
LN_EPS = 1e-5
DEPTH = 1
ALPHA = (2 * DEPTH) ** 0.25
POOL_WINDOWS = (2, 4, 8, 16)
POOL_GROUP_DIM = 256
POOL_WIDTH = 1024
SSD_HEAD_DIM = 64
SSD_WIDTH = 3072
SSD_HEADS = 48
SSD_GROUPS = 8
SSD_STATE = 128
HEADS_PER_GROUP = SSD_HEADS // SSD_GROUPS
GROUP_WIDTH = HEADS_PER_GROUP * SSD_HEAD_DIM
CONV_K = 4
CONV_DIM = SSD_WIDTH + 2 * SSD_GROUPS * SSD_STATE
MIX_WIDTH = POOL_WIDTH + SSD_WIDTH
PROJ_MAIN = POOL_WIDTH + SSD_WIDTH + CONV_DIM
PEER_HEADS = 8
PEER_NKEYS = 128
PEER_TOPK = 16
PEER_HALF = 128

LANES = 128
SUBLANES = 8
CHUNK = 128
CONV_TAIL = SUBLANES
POOL_TAIL = 16
VMEM_LIMIT = 60 * 1024 * 1024


def _layer_norm(x, g, b):
    mu = jnp.mean(x, axis=-1, keepdims=True)
    xc = x - mu
    var = jnp.mean(xc * xc, axis=-1, keepdims=True)
    return xc * lax.rsqrt(var + LN_EPS) * g + b


def _silu(x):
    return x * jax.nn.sigmoid(x)


def _softplus(x):
    return jnp.maximum(x, 0.0) + jnp.log1p(jnp.exp(-jnp.abs(x)))


def _gelu_exact(x):
    sqrt_half = np.float32(np.sqrt(0.5))
    return 0.5 * x * (1.0 + lax.erf(x * sqrt_half))


def _ln_inproj_kernel(x_ref, g_ref, b_ref, w_ref, wdt_ref, o_ref, dt_ref, hb_ref):
    @pl.when(pl.program_id(1) == 0)
    def _():
        h = _layer_norm(x_ref[...], g_ref[...], b_ref[...]).astype(BF16)
        hb_ref[...] = h
        dt_ref[...] = jnp.dot(h, wdt_ref[...], preferred_element_type=F32)

    o_ref[...] = jnp.dot(hb_ref[...], w_ref[...], preferred_element_type=F32)


def _ln_inproj(x2d, g, b, w_main, w_dt, tm, tn):
    t, d = x2d.shape
    n = w_main.shape[1]
    return pl.pallas_call(
        _ln_inproj_kernel,
        grid=(t // tm, n // tn),
        in_specs=[
            pl.BlockSpec((tm, d), lambda i, j: (i, 0)),
            pl.BlockSpec((1, d), lambda i, j: (0, 0)),
            pl.BlockSpec((1, d), lambda i, j: (0, 0)),
            pl.BlockSpec((d, tn), lambda i, j: (0, j)),
            pl.BlockSpec((d, LANES), lambda i, j: (0, 0)),
        ],
        out_specs=[
            pl.BlockSpec((tm, tn), lambda i, j: (i, j)),
            pl.BlockSpec((tm, LANES), lambda i, j: (i, 0)),
        ],
        out_shape=[
            jax.ShapeDtypeStruct((t, n), F32),
            jax.ShapeDtypeStruct((t, LANES), F32),
        ],
        scratch_shapes=[pltpu.VMEM((tm, d), BF16)],
        compiler_params=pltpu.CompilerParams(
            dimension_semantics=("parallel", "arbitrary"), vmem_limit_bytes=VMEM_LIMIT),
        name="ln_inproj",
    )(x2d, g, b, w_main, w_dt)


def _mixer_kernel(z_ref, x_ref, p_ref, bm_ref, cm_ref, dt_ref,
                  state0_ref, ctail0_ref, ptail0_ref,
                  convw_ref, convb_ref, dtb_ref, alog_ref, dskip_ref, normg_ref,
                  poolw_ref, pscale_ref, hexp_ref,
                  mix_ref, state_out_ref, ctail_out_ref, ptail_out_ref,
                  cbuf, act, ybuf, pbuf, state):
    q = CHUNK
    c = pl.program_id(1)

    @pl.when(c == 0)
    def _init():
        state[...] = state0_ref[...]
        cbuf[0:CONV_TAIL, :] = ctail0_ref[...]
        pbuf[0:POOL_TAIL, :] = ptail0_ref[...]

    cbuf[CONV_TAIL:CONV_TAIL + q, 0:SSD_WIDTH] = x_ref[...]
    cbuf[CONV_TAIL:CONV_TAIL + q, SSD_WIDTH:SSD_WIDTH + 1024] = bm_ref[...]
    cbuf[CONV_TAIL:CONV_TAIL + q, SSD_WIDTH + 1024:CONV_DIM] = cm_ref[...]
    strip = 512
    for s in range(0, CONV_DIM, strip):
        acc = convb_ref[:, s:s + strip]
        for k in range(CONV_K):
            r0 = CONV_TAIL - (CONV_K - 1) + k
            acc = acc + convw_ref[k:k + 1, s:s + strip] * cbuf[r0:r0 + q, s:s + strip]
        act[:, s:s + strip] = _silu(acc)
    cbuf[0:CONV_TAIL, :] = cbuf[q:q + CONV_TAIL, :]

    dt = _softplus(dt_ref[...] + dtb_ref[...])
    a_neg = -jnp.exp(alog_ref[...])
    adt = dt * a_neg
    row = lax.broadcasted_iota(jnp.int32, (q, q), 0)
    col = lax.broadcasted_iota(jnp.int32, (q, q), 1)
    causal = row >= col
    tril = jnp.where(causal, 1.0, 0.0).astype(F32)
    acs = jnp.dot(tril, adt, preferred_element_type=F32, precision=lax.Precision.HIGHEST)
    aend = acs[q - 1:q, :]
    wmat = jnp.exp(aend - acs) * dt
    acs_t = acs.T
    dt_t = dt.T
    dec_all = jnp.dot(jnp.broadcast_to(jnp.exp(aend), (SUBLANES, LANES)), hexp_ref[...],
                      preferred_element_type=F32, precision=lax.Precision.HIGHEST)[0:1, :]
    lane = lax.broadcasted_iota(jnp.int32, (q, LANES), 1)
    low_half = lane < SSD_HEAD_DIM

    for g in range(SSD_GROUPS):
        bg = act[:, SSD_WIDTH + g * SSD_STATE:SSD_WIDTH + (g + 1) * SSD_STATE]
        cg = act[:, SSD_WIDTH + 1024 + g * SSD_STATE:SSD_WIDTH + 1024 + (g + 1) * SSD_STATE]
        bgb = bg.astype(BF16)
        cb = lax.dot_general(cg.astype(BF16), bgb, (((1,), (1,)), ((), ())),
                             preferred_element_type=F32)
        st_g = state[g]
        st_b = st_g.astype(BF16)
        xw_parts = []
        for pr in range(HEADS_PER_GROUP // 2):
            h0 = g * HEADS_PER_GROUP + 2 * pr
            c0 = h0 * SSD_HEAD_DIM
            xp = act[:, c0:c0 + LANES]
            xpb = xp.astype(BF16)
            stp = st_b[:, pr * LANES:(pr + 1) * LANES]
            ys = []
            for h in (h0, h0 + 1):
                colb = jnp.broadcast_to(acs[:, h:h + 1], (q, q))
                rowb = jnp.broadcast_to(acs_t[h:h + 1, :], (q, q))
                lmat = jnp.exp(jnp.where(causal, colb - rowb, -jnp.inf))
                m = (cb * lmat * jnp.broadcast_to(dt_t[h:h + 1, :], (q, q))).astype(BF16)
                ce = (cg * jnp.exp(colb)).astype(BF16)
                ys.append(jnp.dot(m, xpb, preferred_element_type=F32)
                          + jnp.dot(ce, stp, preferred_element_type=F32))
            ypair = jnp.where(low_half, ys[0], ys[1]) + dskip_ref[:, c0:c0 + LANES] * xp
            ybuf[:, c0:c0 + LANES] = ypair
            wpair = jnp.where(low_half,
                              jnp.broadcast_to(wmat[:, h0:h0 + 1], (q, LANES)),
                              jnp.broadcast_to(wmat[:, h0 + 1:h0 + 2], (q, LANES)))
            xw_parts.append((xp * wpair).astype(BF16))
        xw = jnp.concatenate(xw_parts, axis=1)
        upd = lax.dot_general(bgb, xw, (((0,), (0,)), ((), ())),
                              preferred_element_type=F32)
        state[g] = st_g * dec_all[:, g * GROUP_WIDTH:(g + 1) * GROUP_WIDTH] + upd

    for g in range(SSD_GROUPS):
        cs = slice(g * GROUP_WIDTH, (g + 1) * GROUP_WIDTH)
        yg = ybuf[:, cs] * _silu(z_ref[:, cs])
        ms = jnp.mean(yg * yg, axis=-1, keepdims=True)
        yn = yg * lax.rsqrt(ms + LN_EPS) * normg_ref[:, cs]
        mix_ref[:, POOL_WIDTH + g * GROUP_WIDTH:POOL_WIDTH + (g + 1) * GROUP_WIDTH] = yn.astype(BF16)

    pbuf[POOL_TAIL:POOL_TAIL + q, :] = p_ref[...]
    for gi, w in enumerate(POOL_WINDOWS):
        cs = slice(gi * POOL_GROUP_DIM, (gi + 1) * POOL_GROUP_DIM)
        cur = pbuf[POOL_TAIL:POOL_TAIL + q, cs]
        acc = cur
        for j in range(1, w):
            acc = acc + pbuf[POOL_TAIL - j:POOL_TAIL - j + q, cs]
        pooled = acc * (1.0 / w) - cur
        mixed = jnp.dot(pooled.astype(BF16), poolw_ref[gi], preferred_element_type=F32)
        mix_ref[:, cs] = (mixed * pscale_ref[:, cs]).astype(BF16)
    pbuf[0:POOL_TAIL, :] = pbuf[q:q + POOL_TAIL, :]

    @pl.when(c == pl.num_programs(1) - 1)
    def _fin():
        state_out_ref[0] = state[...]
        ctail_out_ref[0] = cbuf[0:CONV_TAIL, :]
        ptail_out_ref[0] = pbuf[0:POOL_TAIL, :]


def _mixer(proj, dt_raw, carries, params, bsz, n_chunks):
    state0, ctail0, ptail0 = carries
    q = CHUNK
    t = proj.shape[0]

    def tok(col_block):
        return lambda b, c: (b * n_chunks + c, col_block)

    def const(shape):
        nd = len(shape)
        return pl.BlockSpec(shape, lambda b, c: (0,) * nd)

    (convw, convb, dtb, alog, dskip, normg, poolw, pscale, hexp) = params
    return pl.pallas_call(
        _mixer_kernel,
        grid=(bsz, n_chunks),
        in_specs=[
            pl.BlockSpec((q, SSD_WIDTH), tok(0)),
            pl.BlockSpec((q, SSD_WIDTH), tok(1)),
            pl.BlockSpec((q, 1024), tok(6)),
            pl.BlockSpec((q, 1024), tok(7)),
            pl.BlockSpec((q, 1024), tok(8)),
            pl.BlockSpec((q, LANES), tok(0)),
            const(state0.shape), const(ctail0.shape), const(ptail0.shape),
            const(convw.shape), const(convb.shape), const(dtb.shape), const(alog.shape),
            const(dskip.shape), const(normg.shape), const(poolw.shape), const(pscale.shape),
            const(hexp.shape),
        ],
        out_specs=[
            pl.BlockSpec((q, MIX_WIDTH), tok(0)),
            pl.BlockSpec((1,) + state0.shape, lambda b, c: (b, 0, 0, 0)),
            pl.BlockSpec((1,) + ctail0.shape, lambda b, c: (b, 0, 0)),
            pl.BlockSpec((1,) + ptail0.shape, lambda b, c: (b, 0, 0)),
        ],
        out_shape=[
            jax.ShapeDtypeStruct((t, MIX_WIDTH), BF16),
            jax.ShapeDtypeStruct((bsz,) + state0.shape, F32),
            jax.ShapeDtypeStruct((bsz,) + ctail0.shape, F32),
            jax.ShapeDtypeStruct((bsz,) + ptail0.shape, F32),
        ],
        scratch_shapes=[
            pltpu.VMEM((q + CONV_TAIL, CONV_DIM), F32),
            pltpu.VMEM((q, CONV_DIM), F32),
            pltpu.VMEM((q, SSD_WIDTH), F32),
            pltpu.VMEM((q + POOL_TAIL, POOL_WIDTH), F32),
            pltpu.VMEM(state0.shape, F32),
        ],
        compiler_params=pltpu.CompilerParams(
            dimension_semantics=("parallel", "arbitrary"), vmem_limit_bytes=VMEM_LIMIT),
        name="mixer",
    )(proj, proj, proj, proj, proj, dt_raw, state0, ctail0, ptail0,
      convw, convb, dtb, alog, dskip, normg, poolw, pscale, hexp)


def _outproj_ln_kernel(mix_ref, w_ref, x_ref, gin_ref, bin_ref, g1_ref, b1_ref,
                       h1_ref, h1t_ref, acc_ref):
    k = pl.program_id(1)

    @pl.when(k == 0)
    def _():
        acc_ref[...] = jnp.zeros_like(acc_ref)

    acc_ref[...] += jnp.dot(mix_ref[...], w_ref[...], preferred_element_type=F32)

    @pl.when(k == pl.num_programs(1) - 1)
    def _():
        h0 = _layer_norm(x_ref[...], gin_ref[...], bin_ref[...])
        h1 = _layer_norm(ALPHA * h0 + acc_ref[...], g1_ref[...], b1_ref[...])
        h1_ref[...] = h1
        h1t_ref[...] = h1.T.astype(BF16)


def _outproj_ln(mix, w_out, x2d, gin, bin_, g1, b1, tm, tk):
    t, d = x2d.shape
    kdim = mix.shape[1]
    vec = pl.BlockSpec((1, d), lambda i, k: (0, 0))
    return pl.pallas_call(
        _outproj_ln_kernel,
        grid=(t // tm, kdim // tk),
        in_specs=[
            pl.BlockSpec((tm, tk), lambda i, k: (i, k)),
            pl.BlockSpec((tk, d), lambda i, k: (k, 0)),
            pl.BlockSpec((tm, d), lambda i, k: (i, 0)),
            vec, vec, vec, vec,
        ],
        out_specs=[
            pl.BlockSpec((tm, d), lambda i, k: (i, 0)),
            pl.BlockSpec((d, tm), lambda i, k: (0, i)),
        ],
        out_shape=[
            jax.ShapeDtypeStruct((t, d), F32),
            jax.ShapeDtypeStruct((d, t), BF16),
        ],
        scratch_shapes=[pltpu.VMEM((tm, d), F32)],
        compiler_params=pltpu.CompilerParams(
            dimension_semantics=("parallel", "arbitrary"), vmem_limit_bytes=VMEM_LIMIT),
        name="outproj_ln",
    )(mix, w_out, x2d, gin, bin_, g1, b1)


_CAND_GROUPS = 10


def _cand_valid_rows(gidx):
    if gidx <= 1:
        return 8
    if gidx == 9:
        return 8
    k1 = gidx - 1
    return PEER_TOPK // (k1 + 1)


def _peer_route_kernel(h1t_ref, wqt_ref, keys_ref, a_ref, n1_ref, b_ref, r2_ref,
                       q_ref, sorig, swork, rank, stop):
    tn = h1t_ref.shape[1]
    nk = PEER_NKEYS
    q_ref[...] = jnp.dot(wqt_ref[...], h1t_ref[...], preferred_element_type=F32).astype(BF16)
    key_iota = lax.broadcasted_iota(jnp.int32, (nk, tn), 0).astype(F32)
    sub_iota = lax.broadcasted_iota(jnp.int32, (SUBLANES, tn), 0)

    def head_body(h, carry):
        for side in range(2):
            r0 = pl.multiple_of((h * 2 + side) * PEER_HALF, PEER_HALF)
            s = jnp.dot(keys_ref[side], q_ref[pl.ds(r0, PEER_HALF), :],
                        preferred_element_type=F32)
            sorig[side] = s
            swork[side] = s
            rank[side] = jnp.full((nk, tn), float(PEER_TOPK), F32)

            def round_body(k, c2, side=side):
                sw = swork[side]
                m = jnp.max(sw, axis=0, keepdims=True)
                first = jnp.min(jnp.where(sw == m, key_iota, float(nk)), axis=0, keepdims=True)
                onehot = key_iota == first
                rank[side] = jnp.where(onehot, k.astype(F32), rank[side])
                swork[side] = jnp.where(onehot, -jnp.inf, sw)
                stop[side, pl.ds(k, 1), :] = m
                return c2

            lax.fori_loop(0, PEER_TOPK, round_body, 0)

        st1 = stop[0]
        st2 = stop[1]
        top1 = st1[0:1, :]
        top2 = st2[0:1, :]
        cmax = top1 + top2

        def bc(rowv):
            return jnp.broadcast_to(rowv, (SUBLANES, tn))

        groups = [st2[0:8, :] + bc(top1), st2[8:16, :] + bc(top1)]
        for k1 in range(1, 8):
            groups.append(st2[0:8, :] + bc(st1[k1:k1 + 1, :]))
        groups.append(st1[8:16, :] + bc(top2))

        counts = [jnp.zeros((SUBLANES, tn), F32) for _ in range(_CAND_GROUPS)]
        for gp in range(_CAND_GROUPS):
            for rp in range(_cand_valid_rows(gp)):
                cp = bc(groups[gp][rp:rp + 1, :])
                for gq in range(_CAND_GROUPS):
                    if gq > gp:
                        beats = cp >= groups[gq]
                    elif gq < gp:
                        beats = cp > groups[gq]
                    else:
                        beats = jnp.logical_or(
                            cp > groups[gq],
                            jnp.logical_and(cp == groups[gq], sub_iota > rp))
                    counts[gq] = counts[gq] + jnp.where(beats, 1.0, 0.0)
        sel = []
        zsum = jnp.zeros((SUBLANES, tn), F32)
        for gq in range(_CAND_GROUPS):
            ok = jnp.logical_and(counts[gq] < float(PEER_TOPK), sub_iota < _cand_valid_rows(gq))
            sg = jnp.where(ok, 1.0, 0.0)
            sel.append(sg)
            zsum = zsum + sg * jnp.exp(groups[gq] - bc(cmax))
        inv_z = 1.0 / jnp.sum(zsum, axis=0, keepdims=True)

        n_rows = [jnp.sum(sel[0] + sel[1], axis=0, keepdims=True)]
        for k1 in range(1, 8):
            n_rows.append(jnp.sum(sel[k1 + 1], axis=0, keepdims=True))
        for k1 in range(8, 16):
            n_rows.append(sel[9][k1 - 8:k1 - 7, :])

        r1 = rank[0]
        n1 = jnp.zeros((nk, tn), F32)
        for k1 in range(PEER_TOPK):
            n1 = jnp.where(r1 == float(k1), jnp.broadcast_to(n_rows[k1], (nk, tn)), n1)

        a_ref[h] = jnp.exp(sorig[0] - top1) * inv_z
        n1_ref[h] = n1
        b_ref[h] = jnp.exp(sorig[1] - top2)
        r2_ref[h] = rank[1]
        return carry

    lax.fori_loop(0, PEER_HEADS, head_body, 0)


def _peer_route(h1t, wqt, keys, tn):
    d, t = h1t.shape
    nk = PEER_NKEYS
    route_spec = pl.BlockSpec((PEER_HEADS, nk, tn), lambda i: (0, 0, i))
    route_shape = jax.ShapeDtypeStruct((PEER_HEADS, nk, t), F32)
    return pl.pallas_call(
        _peer_route_kernel,
        grid=(t // tn,),
        in_specs=[
            pl.BlockSpec((d, tn), lambda i: (0, i)),
            pl.BlockSpec(wqt.shape, lambda i: (0, 0)),
            pl.BlockSpec(keys.shape, lambda i: (0, 0, 0)),
        ],
        out_specs=[route_spec] * 4,
        out_shape=[route_shape] * 4,
        scratch_shapes=[
            pltpu.VMEM((wqt.shape[0], tn), BF16),
            pltpu.VMEM((2, nk, tn), F32),
            pltpu.VMEM((2, nk, tn), F32),
            pltpu.VMEM((2, nk, tn), F32),
            pltpu.VMEM((2, PEER_TOPK, tn), F32),
        ],
        compiler_params=pltpu.CompilerParams(
            dimension_semantics=("parallel",), vmem_limit_bytes=VMEM_LIMIT),
        name="peer_route",
    )(h1t, wqt, keys)


def _peer_dense_kernel(xt_ref, u_ref, vt_ref, a_ref, n1_ref, b_ref, r2_ref,
                       h1_ref, g2_ref, b2_ref, o_ref, acc_ref, st_ref, at_ref):
    e = pl.program_id(1)
    ec = u_ref.shape[0]
    tm = xt_ref.shape[1]
    rows = 16
    n_sub = ec // PEER_NKEYS

    @pl.when(e == 0)
    def _():
        acc_ref[...] = jnp.zeros_like(acc_ref)

    st_ref[...] = jnp.dot(u_ref[...], xt_ref[...], preferred_element_type=F32)

    for j in range(n_sub):
        i1 = e * n_sub + j
        arow = [a_ref[h, pl.ds(i1, 1), :] for h in range(PEER_HEADS)]
        nrow = [n1_ref[h, pl.ds(i1, 1), :] for h in range(PEER_HEADS)]

        def blk(rb, carry, j=j, arow=arow, nrow=nrow):
            r0 = pl.multiple_of(rb * rows, rows)
            w = jnp.zeros((rows, tm), F32)
            for h in range(PEER_HEADS):
                r2 = r2_ref[h, pl.ds(r0, rows), :]
                bb = b_ref[h, pl.ds(r0, rows), :]
                w = w + arow[h] * jnp.where(r2 < nrow[h], bb, 0.0)
            s = st_ref[pl.ds(j * PEER_NKEYS + r0, rows), :]
            at_ref[pl.ds(j * PEER_NKEYS + r0, rows), :] = (w * _gelu_exact(s)).astype(BF16)
            return carry

        lax.fori_loop(0, PEER_NKEYS // rows, blk, 0)

    acc_ref[...] += jnp.dot(vt_ref[...], at_ref[...], preferred_element_type=F32)

    @pl.when(e == pl.num_programs(1) - 1)
    def _():
        r = ALPHA * h1_ref[...] + acc_ref[...].T
        o_ref[...] = _layer_norm(r, g2_ref[...], b2_ref[...])


def _peer_dense(h1t, u_b, vt_b, routes, h1, g2, b2, tm, ec):
    d, t = h1t.shape
    n_exp = u_b.shape[0]
    a_t, n1_t, b_t, r2_t = routes
    route_spec = pl.BlockSpec((PEER_HEADS, PEER_NKEYS, tm), lambda i, e: (0, 0, i))
    vec = pl.BlockSpec((1, d), lambda i, e: (0, 0))
    return pl.pallas_call(
        _peer_dense_kernel,
        grid=(t // tm, n_exp // ec),
        in_specs=[
            pl.BlockSpec((d, tm), lambda i, e: (0, i)),
            pl.BlockSpec((ec, d), lambda i, e: (e, 0)),
            pl.BlockSpec((d, ec), lambda i, e: (0, e)),
            route_spec, route_spec, route_spec, route_spec,
            pl.BlockSpec((tm, d), lambda i, e: (i, 0)),
            vec, vec,
        ],
        out_specs=pl.BlockSpec((tm, d), lambda i, e: (i, 0)),
        out_shape=jax.ShapeDtypeStruct((t, d), F32),
        scratch_shapes=[
            pltpu.VMEM((d, tm), F32),
            pltpu.VMEM((ec, tm), F32),
            pltpu.VMEM((ec, tm), BF16),
        ],
        compiler_params=pltpu.CompilerParams(
            dimension_semantics=("parallel", "arbitrary"), vmem_limit_bytes=VMEM_LIMIT),
        name="peer_dense",
    )(h1t, u_b, vt_b, a_t, n1_t, b_t, r2_t, h1, g2, b2)


def _pick(n, prefs):
    for p in prefs:
        if n % p == 0:
            return p
    return n


def kernel(x, meta_tokens, ln_in_g, ln_in_b, w_in, pool_w, pool_scale, conv_w, conv_b, dt_bias,
           A_log, D_skip, ssd_norm_g, w_out, ln1_g, ln1_b, peer_wq, peer_keys, peer_u, peer_v,
           ln2_g, ln2_b):
    bsz, seq, d = x.shape
    assert seq % CHUNK == 0 and w_in.shape[0] == DEPTH
    t = bsz * seq
    n_chunks = seq // CHUNK

    w0 = w_in[0]
    o1, o2, o3 = POOL_WIDTH, POOL_WIDTH + SSD_WIDTH, POOL_WIDTH + SSD_WIDTH + CONV_DIM
    w_main = jnp.concatenate(
        [w0[:, o1:o2], w0[:, o2:o2 + SSD_WIDTH], w0[:, :o1], w0[:, o2 + SSD_WIDTH:o3]],
        axis=1).astype(BF16)
    w_dt = jnp.pad(w0[:, o3:], ((0, 0), (0, LANES - SSD_HEADS))).astype(BF16)
    g_in = ln_in_g.reshape(1, d)
    b_in = ln_in_b.reshape(1, d)
    pad_h = (0, LANES - SSD_HEADS)
    head_of_col = np.arange(SSD_WIDTH) // SSD_HEAD_DIM
    hexp = jnp.asarray((np.arange(LANES)[:, None] == head_of_col[None, :]).astype(np.float32))
    mixer_params = (
        conv_w[0], conv_b[0].reshape(1, CONV_DIM),
        jnp.pad(dt_bias[0], pad_h).reshape(1, LANES), jnp.pad(A_log[0], pad_h).reshape(1, LANES),
        jnp.repeat(D_skip[0], SSD_HEAD_DIM).reshape(1, SSD_WIDTH),
        ssd_norm_g[0].reshape(1, SSD_WIDTH),
        pool_w[0].astype(BF16), pool_scale[0].reshape(1, POOL_WIDTH), hexp)
    w_out_b = w_out[0].astype(BF16)
    wqt = peer_wq[0].T.astype(BF16)
    keys_b = peer_keys[0].astype(BF16)
    u_b = peer_u[0].astype(BF16)
    vt_b = peer_v[0].T.astype(BF16)

    proj_m, dt_m = _ln_inproj(meta_tokens, g_in, b_in, w_main, w_dt, N_META, _pick(PROJ_MAIN, (1024,)))
    n_pad = CHUNK - N_META
    proj_m = jnp.pad(proj_m, ((n_pad, 0), (0, 0)))
    dt_m = jnp.pad(dt_m, ((n_pad, 0), (0, 0)), constant_values=-1e30)
    zero_carries = (jnp.zeros((SSD_GROUPS, SSD_STATE, GROUP_WIDTH), F32),
                    jnp.zeros((CONV_TAIL, CONV_DIM), F32),
                    jnp.zeros((POOL_TAIL, POOL_WIDTH), F32))
    _, st_m, ct_m, pt_m = _mixer(proj_m, dt_m, zero_carries, mixer_params, 1, 1)
    carries = (st_m[0], ct_m[0], pt_m[0])

    x2d = x.reshape(t, d)
    tm1 = _pick(t, (1024, 512, 256, 128))
    proj, dt_raw = _ln_inproj(x2d, g_in, b_in, w_main, w_dt, tm1, _pick(PROJ_MAIN, (1024,)))
    mix, _, _, _ = _mixer(proj, dt_raw, carries, mixer_params, bsz, n_chunks)
    tm3 = _pick(t, (512, 256, 128))
    h1, h1t = _outproj_ln(mix, w_out_b, x2d, g_in, b_in, ln1_g[0].reshape(1, d),
                          ln1_b[0].reshape(1, d), tm3, 1024)
    tn4 = _pick(t, (512, 256, 128))
    routes = _peer_route(h1t, wqt, keys_b, tn4)
    tm5 = _pick(t, (512, 256, 128))
    out = _peer_dense(h1t, u_b, vt_b, routes, h1, ln2_g[0].reshape(1, d), ln2_b[0].reshape(1, d),
                      tm5, 512)
    return out.reshape(bsz, seq, d)
```

```python
import functools
import math

import numpy as np
import jax
import jax.numpy as jnp
from jax import lax
from jax.experimental import pallas as pl
from jax.experimental.pallas import tpu as pltpu

F32 = jnp.float32
BF16 = jnp.bfloat16

N_META = 16
LN_EPS = 1e-5
DEPTH = 1
ALPHA = (2 * DEPTH) ** 0.25
POOL_WINDOWS = (2, 4, 8, 16)
POOL_GROUP_DIM = 256
POOL_WIDTH = 1024
SSD_HEAD_DIM = 64
SSD_WIDTH = 3072
SSD_HEADS = 48
SSD_GROUPS = 8
SSD_STATE = 128
HEADS_PER_GROUP = SSD_HEADS // SSD_GROUPS
GROUP_WIDTH = HEADS_PER_GROUP * SSD_HEAD_DIM
CONV_K = 4
CONV_DIM = SSD_WIDTH + 2 * SSD_GROUPS * SSD_STATE
MIX_WIDTH = POOL_WIDTH + SSD_WIDTH
PROJ_MAIN = POOL_WIDTH + SSD_WIDTH + CONV_DIM
PEER_HEADS = 8
PEER_NKEYS = 128
PEER_TOPK = 16
PEER_HALF = 128
PEER_CHUNK = 512
PEER_SUBS = PEER_CHUNK // PEER_NKEYS

LANES = 128
SUBLANES = 8
BF16_ROWS = 2 * SUBLANES
CHUNK = 128
CONV_TAIL = SUBLANES
POOL_TAIL = 16
VMEM_LIMIT = 60 * 1024 * 1024


def _layer_norm(x, g, b):
    mu = jnp.mean(x, axis=-1, keepdims=True)
    xc = x - mu
    var = jnp.mean(xc * xc, axis=-1, keepdims=True)
    return xc * lax.rsqrt(var + LN_EPS) * g + b


def _silu(x):
    return x * jax.nn.sigmoid(x)


def _softplus(x):
    return jnp.maximum(x, 0.0) + jnp.log1p(jnp.exp(-jnp.abs(x)))


def _gelu_exact(x):
    sqrt_half = np.float32(np.sqrt(0.5))
    return 0.5 * x * (1.0 + lax.erf(x * sqrt_half))


def _ln_inproj_kernel(x_ref, g_ref, b_ref, w_ref, wdt_ref, o_ref, dt_ref, hb_ref):
    @pl.when(pl.program_id(1) == 0)
    def _():
        h = _layer_norm(x_ref[...], g_ref[...], b_ref[...]).astype(BF16)
        hb_ref[...] = h
        dt_ref[...] = jnp.dot(h, wdt_ref[...], preferred_element_type=F32)

    o_ref[...] = jnp.dot(hb_ref[...], w_ref[...], preferred_element_type=F32)


def _ln_inproj(x2d, g, b, w_main, w_dt, tm, tn):
    t, d = x2d.shape
    n = w_main.shape[1]
    return pl.pallas_call(
        _ln_inproj_kernel,
        grid=(t // tm, n // tn),
        in_specs=[
            pl.BlockSpec((tm, d), lambda i, j: (i, 0)),
            pl.BlockSpec((1, d), lambda i, j: (0, 0)),
            pl.BlockSpec((1, d), lambda i, j: (0, 0)),
            pl.BlockSpec((d, tn), lambda i, j: (0, j)),
            pl.BlockSpec((d, LANES), lambda i, j: (0, 0)),
        ],
        out_specs=[
            pl.BlockSpec((tm, tn), lambda i, j: (i, j)),
            pl.BlockSpec((tm, LANES), lambda i, j: (i, 0)),
        ],
        out_shape=[
            jax.ShapeDtypeStruct((t, n), F32),
            jax.ShapeDtypeStruct((t, LANES), F32),
        ],
        scratch_shapes=[pltpu.VMEM((tm, d), BF16)],
        compiler_params=pltpu.CompilerParams(
            dimension_semantics=("parallel", "arbitrary"), vmem_limit_bytes=VMEM_LIMIT),
        name="ln_inproj",
    )(x2d, g, b, w_main, w_dt)


def _mixer_kernel(z_ref, x_ref, p_ref, bm_ref, cm_ref, dt_ref,
                  state0_ref, ctail0_ref, ptail0_ref,
                  convw_ref, convb_ref, dtb_ref, alog_ref, dskip_ref, normg_ref,
                  poolw_ref, pscale_ref, hexp_ref,
                  mix_ref, state_out_ref, ctail_out_ref, ptail_out_ref,
                  cbuf, act, ybuf, pbuf, state):
    q = CHUNK
    c = pl.program_id(1)

    @pl.when(c == 0)
    def _init():
        state[...] = state0_ref[...]
        cbuf[0:CONV_TAIL, :] = ctail0_ref[...]
        pbuf[0:POOL_TAIL, :] = ptail0_ref[...]

    cbuf[CONV_TAIL:CONV_TAIL + q, 0:SSD_WIDTH] = x_ref[...]
    cbuf[CONV_TAIL:CONV_TAIL + q, SSD_WIDTH:SSD_WIDTH + 1024] = bm_ref[...]
    cbuf[CONV_TAIL:CONV_TAIL + q, SSD_WIDTH + 1024:CONV_DIM] = cm_ref[...]
    strip = 512
    for s in range(0, CONV_DIM, strip):
        acc = convb_ref[:, s:s + strip]
        for k in range(CONV_K):
            r0 = CONV_TAIL - (CONV_K - 1) + k
            acc = acc + convw_ref[k:k + 1, s:s + strip] * cbuf[r0:r0 + q, s:s + strip]
        act[:, s:s + strip] = _silu(acc)
    cbuf[0:CONV_TAIL, :] = cbuf[q:q + CONV_TAIL, :]

    dt = _softplus(dt_ref[...] + dtb_ref[...])
    a_neg = -jnp.exp(alog_ref[...])
    adt = dt * a_neg
    row = lax.broadcasted_iota(jnp.int32, (q, q), 0)
    col = lax.broadcasted_iota(jnp.int32, (q, q), 1)
    causal = row >= col
    tril = jnp.where(causal, 1.0, 0.0).astype(F32)
    acs = jnp.dot(tril, adt, preferred_element_type=F32, precision=lax.Precision.HIGHEST)
    aend = acs[q - 1:q, :]
    wmat = jnp.exp(aend - acs) * dt
    acs_t = acs.T
    dt_t = dt.T
    dec_all = jnp.dot(jnp.broadcast_to(jnp.exp(aend), (SUBLANES, LANES)), hexp_ref[...],
                      preferred_element_type=F32, precision=lax.Precision.HIGHEST)[0:1, :]
    lane = lax.broadcasted_iota(jnp.int32, (q, LANES), 1)
    low_half = lane < SSD_HEAD_DIM

    for g in range(SSD_GROUPS):
        bg = act[:, SSD_WIDTH + g * SSD_STATE:SSD_WIDTH + (g + 1) * SSD_STATE]
        cg = act[:, SSD_WIDTH + 1024 + g * SSD_STATE:SSD_WIDTH + 1024 + (g + 1) * SSD_STATE]
        bgb = bg.astype(BF16)
        cb = lax.dot_general(cg.astype(BF16), bgb, (((1,), (1,)), ((), ())),
                             preferred_element_type=F32)
        st_g = state[g]
        st_b = st_g.astype(BF16)
        xw_parts = []
        for pr in range(HEADS_PER_GROUP // 2):
            h0 = g * HEADS_PER_GROUP + 2 * pr
            c0 = h0 * SSD_HEAD_DIM
            xp = act[:, c0:c0 + LANES]
            xpb = xp.astype(BF16)
            stp = st_b[:, pr * LANES:(pr + 1) * LANES]
            ys = []
            for h in (h0, h0 + 1):
                colb = jnp.broadcast_to(acs[:, h:h + 1], (q, q))
                rowb = jnp.broadcast_to(acs_t[h:h + 1, :], (q, q))
                lmat = jnp.exp(jnp.where(causal, colb - rowb, -jnp.inf))
                m = (cb * lmat * jnp.broadcast_to(dt_t[h:h + 1, :], (q, q))).astype(BF16)
                ce = (cg * jnp.exp(colb)).astype(BF16)
                ys.append(jnp.dot(m, xpb, preferred_element_type=F32)
                          + jnp.dot(ce, stp, preferred_element_type=F32))
            ypair = jnp.where(low_half, ys[0], ys[1]) + dskip_ref[:, c0:c0 + LANES] * xp
            ybuf[:, c0:c0 + LANES] = ypair
            wpair = jnp.where(low_half,
                              jnp.broadcast_to(wmat[:, h0:h0 + 1], (q, LANES)),
                              jnp.broadcast_to(wmat[:, h0 + 1:h0 + 2], (q, LANES)))
            xw_parts.append((xp * wpair).astype(BF16))
        xw = jnp.concatenate(xw_parts, axis=1)
        upd = lax.dot_general(bgb, xw, (((0,), (0,)), ((), ())),
                              preferred_element_type=F32)
        state[g] = st_g * dec_all[:, g * GROUP_WIDTH:(g + 1) * GROUP_WIDTH] + upd

    for g in range(SSD_GROUPS):
        cs = slice(g * GROUP_WIDTH, (g + 1) * GROUP_WIDTH)
        yg = ybuf[:, cs] * _silu(z_ref[:, cs])
        ms = jnp.mean(yg * yg, axis=-1, keepdims=True)
        yn = yg * lax.rsqrt(ms + LN_EPS) * normg_ref[:, cs]
        mix_ref[:, POOL_WIDTH + g * GROUP_WIDTH:POOL_WIDTH + (g + 1) * GROUP_WIDTH] = yn.astype(BF16)

    pbuf[POOL_TAIL:POOL_TAIL + q, :] = p_ref[...]
    for gi, w in enumerate(POOL_WINDOWS):
        cs = slice(gi * POOL_GROUP_DIM, (gi + 1) * POOL_GROUP_DIM)
        cur = pbuf[POOL_TAIL:POOL_TAIL + q, cs]
        acc = cur
        for j in range(1, w):
            acc = acc + pbuf[POOL_TAIL - j:POOL_TAIL - j + q, cs]
        pooled = acc * (1.0 / w) - cur
        mixed = jnp.dot(pooled.astype(BF16), poolw_ref[gi], preferred_element_type=F32)
        mix_ref[:, cs] = (mixed * pscale_ref[:, cs]).astype(BF16)
    pbuf[0:POOL_TAIL, :] = pbuf[q:q + POOL_TAIL, :]

    @pl.when(c == pl.num_programs(1) - 1)
    def _fin():
        state_out_ref[0] = state[...]
        ctail_out_ref[0] = cbuf[0:CONV_TAIL, :]
        ptail_out_ref[0] = pbuf[0:POOL_TAIL, :]


def _mixer(proj, dt_raw, carries, params, bsz, n_chunks):
    state0, ctail0, ptail0 = carries
    q = CHUNK
    t = proj.shape[0]

    def tok(col_block):
        return lambda b, c: (b * n_chunks + c, col_block)

    def const(shape):
        nd = len(shape)
        return pl.BlockSpec(shape, lambda b, c: (0,) * nd)

    (convw, convb, dtb, alog, dskip, normg, poolw, pscale, hexp) = params
    return pl.pallas_call(
        _mixer_kernel,
        grid=(bsz, n_chunks),
        in_specs=[
            pl.BlockSpec((q, SSD_WIDTH), tok(0)),
            pl.BlockSpec((q, SSD_WIDTH), tok(1)),
            pl.BlockSpec((q, 1024), tok(6)),
            pl.BlockSpec((q, 1024), tok(7)),
            pl.BlockSpec((q, 1024), tok(8)),
            pl.BlockSpec((q, LANES), tok(0)),
            const(state0.shape), const(ctail0.shape), const(ptail0.shape),
            const(convw.shape), const(convb.shape), const(dtb.shape), const(alog.shape),
            const(dskip.shape), const(normg.shape), const(poolw.shape), const(pscale.shape),
            const(hexp.shape),
        ],
        out_specs=[
            pl.BlockSpec((q, MIX_WIDTH), tok(0)),
            pl.BlockSpec((1,) + state0.shape, lambda b, c: (b, 0, 0, 0)),
            pl.BlockSpec((1,) + ctail0.shape, lambda b, c: (b, 0, 0)),
            pl.BlockSpec((1,) + ptail0.shape, lambda b, c: (b, 0, 0)),
        ],
        out_shape=[
            jax.ShapeDtypeStruct((t, MIX_WIDTH), BF16),
            jax.ShapeDtypeStruct((bsz,) + state0.shape, F32),
            jax.ShapeDtypeStruct((bsz,) + ctail0.shape, F32),
            jax.ShapeDtypeStruct((bsz,) + ptail0.shape, F32),
        ],
        scratch_shapes=[
            pltpu.VMEM((q + CONV_TAIL, CONV_DIM), F32),
            pltpu.VMEM((q, CONV_DIM), F32),
            pltpu.VMEM((q, SSD_WIDTH), F32),
            pltpu.VMEM((q + POOL_TAIL, POOL_WIDTH), F32),
            pltpu.VMEM(state0.shape, F32),
        ],
        compiler_params=pltpu.CompilerParams(
            dimension_semantics=("parallel", "arbitrary"), vmem_limit_bytes=VMEM_LIMIT),
        name="mixer",
    )(proj, proj, proj, proj, proj, dt_raw, state0, ctail0, ptail0,
      convw, convb, dtb, alog, dskip, normg, poolw, pscale, hexp)


def _outproj_ln_kernel(mix_ref, w_ref, x_ref, gin_ref, bin_ref, g1_ref, b1_ref,
                       h1_ref, h1t_ref, acc_ref):
    k = pl.program_id(1)

    @pl.when(k == 0)
    def _():
        acc_ref[...] = jnp.zeros_like(acc_ref)

    acc_ref[...] += jnp.dot(mix_ref[...], w_ref[...], preferred_element_type=F32)

    @pl.when(k == pl.num_programs(1) - 1)
    def _():
        h0 = _layer_norm(x_ref[...], gin_ref[...], bin_ref[...])
        h1 = _layer_norm(ALPHA * h0 + acc_ref[...], g1_ref[...], b1_ref[...])
        h1_ref[...] = h1
        h1t_ref[...] = h1.T.astype(BF16)


def _outproj_ln(mix, w_out, x2d, gin, bin_, g1, b1, tm, tk):
    t, d = x2d.shape
    kdim = mix.shape[1]
    vec = pl.BlockSpec((1, d), lambda i, k: (0, 0))
    return pl.pallas_call(
        _outproj_ln_kernel,
        grid=(t // tm, kdim // tk),
        in_specs=[
            pl.BlockSpec((tm, tk), lambda i, k: (i, k)),
            pl.BlockSpec((tk, d), lambda i, k: (k, 0)),
            pl.BlockSpec((tm, d), lambda i, k: (i, 0)),
            vec, vec, vec, vec,
        ],
        out_specs=[
            pl.BlockSpec((tm, d), lambda i, k: (i, 0)),
            pl.BlockSpec((d, tm), lambda i, k: (0, i)),
        ],
        out_shape=[
            jax.ShapeDtypeStruct((t, d), F32),
            jax.ShapeDtypeStruct((d, t), BF16),
        ],
        scratch_shapes=[pltpu.VMEM((tm, d), F32)],
        compiler_params=pltpu.CompilerParams(
            dimension_semantics=("parallel", "arbitrary"), vmem_limit_bytes=VMEM_LIMIT),
        name="outproj_ln",
    )(mix, w_out, x2d, gin, bin_, g1, b1)


_CAND_GROUPS = 10


def _cand_valid_rows(gidx):
    if gidx <= 1:
        return 8
    if gidx == 9:
        return 8
    k1 = gidx - 1
    return PEER_TOPK // (k1 + 1)


def _peer_route_kernel(h1t_ref, wqt_ref, keys_ref, a_ref, n1_ref, br_ref,
                       q_ref, sorig, swork, rank, stop):
    tn = h1t_ref.shape[1]
    nk = PEER_NKEYS
    q_ref[...] = jnp.dot(wqt_ref[...], h1t_ref[...], preferred_element_type=F32).astype(BF16)
    key_iota = lax.broadcasted_iota(jnp.int32, (nk, tn), 0).astype(F32)
    sub_iota = lax.broadcasted_iota(jnp.int32, (SUBLANES, tn), 0)

    def head_body(h, carry):
        for side in range(2):
            r0 = pl.multiple_of((h * 2 + side) * PEER_HALF, PEER_HALF)
            s = jnp.dot(keys_ref[side], q_ref[pl.ds(r0, PEER_HALF), :],
                        preferred_element_type=F32)
            sorig[side] = s
            swork[side] = s
            rank[side] = jnp.full((nk, tn), float(PEER_TOPK), F32)

            def round_body(k, c2, side=side):
                sw = swork[side]
                m = jnp.max(sw, axis=0, keepdims=True)
                first = jnp.min(jnp.where(sw == m, key_iota, float(nk)), axis=0, keepdims=True)
                onehot = key_iota == first
                rank[side] = jnp.where(onehot, k.astype(F32), rank[side])
                swork[side] = jnp.where(onehot, -jnp.inf, sw)
                stop[side, pl.ds(k, 1), :] = m
                return c2

            lax.fori_loop(0, PEER_TOPK, round_body, 0)

        st1 = stop[0]
        st2 = stop[1]
        top1 = st1[0:1, :]
        top2 = st2[0:1, :]
        cmax = top1 + top2

        def bc(rowv):
            return jnp.broadcast_to(rowv, (SUBLANES, tn))

        groups = [st2[0:8, :] + bc(top1), st2[8:16, :] + bc(top1)]
        for k1 in range(1, 8):
            groups.append(st2[0:8, :] + bc(st1[k1:k1 + 1, :]))
        groups.append(st1[8:16, :] + bc(top2))

        counts = [jnp.zeros((SUBLANES, tn), F32) for _ in range(_CAND_GROUPS)]
        for gp in range(_CAND_GROUPS):
            for rp in range(_cand_valid_rows(gp)):
                cp = bc(groups[gp][rp:rp + 1, :])
                for gq in range(_CAND_GROUPS):
                    if gq > gp:
                        beats = cp >= groups[gq]
                    elif gq < gp:
                        beats = cp > groups[gq]
                    else:
                        beats = jnp.logical_or(
                            cp > groups[gq],
                            jnp.logical_and(cp == groups[gq], sub_iota > rp))
                    counts[gq] = counts[gq] + jnp.where(beats, 1.0, 0.0)
        sel = []
        zsum = jnp.zeros((SUBLANES, tn), F32)
        for gq in range(_CAND_GROUPS):
            ok = jnp.logical_and(counts[gq] < float(PEER_TOPK), sub_iota < _cand_valid_rows(gq))
            sg = jnp.where(ok, 1.0, 0.0)
            sel.append(sg)
            zsum = zsum + sg * jnp.exp(groups[gq] - bc(cmax))
        inv_z = 1.0 / jnp.sum(zsum, axis=0, keepdims=True)

        n_rows = [jnp.sum(sel[0] + sel[1], axis=0, keepdims=True)]
        for k1 in range(1, 8):
            n_rows.append(jnp.sum(sel[k1 + 1], axis=0, keepdims=True))
        for k1 in range(8, 16):
            n_rows.append(sel[9][k1 - 8:k1 - 7, :])

        r1 = rank[0]
        n1 = jnp.zeros((nk, tn), F32)
        for k1 in range(PEER_TOPK):
            n1 = jnp.where(r1 == float(k1), jnp.broadcast_to(n_rows[k1], (nk, tn)), n1)

        a_ref[h] = jnp.exp(sorig[0] - top1) * inv_z
        n1_ref[h] = n1
        packed = (pltpu.bitcast(jnp.exp(sorig[1] - top2).astype(BF16), jnp.uint32),
                  pltpu.bitcast(rank[1].astype(BF16), jnp.uint32))
        for which in range(2):
            for lb in range(tn // LANES):
                for rb in range(nk // BF16_ROWS):
                    br_ref[lb, rb, h, which] = packed[which][
                        rb * SUBLANES:(rb + 1) * SUBLANES, lb * LANES:(lb + 1) * LANES]
        return carry

    lax.fori_loop(0, PEER_HEADS, head_body, 0)


def _peer_route(h1t, wqt, keys, tn):
    d, t = h1t.shape
    nk = PEER_NKEYS
    route_spec = pl.BlockSpec((PEER_HEADS, nk, tn), lambda i: (0, 0, i))
    route_f32 = jax.ShapeDtypeStruct((PEER_HEADS, nk, t), F32)
    tile_dims = (nk // BF16_ROWS, PEER_HEADS, 2, SUBLANES, LANES)
    tile_spec = pl.BlockSpec((tn // LANES,) + tile_dims, lambda i: (i, 0, 0, 0, 0, 0))
    route_b16 = jax.ShapeDtypeStruct((t // LANES,) + tile_dims, jnp.uint32)
    return pl.pallas_call(
        _peer_route_kernel,
        grid=(t // tn,),
        in_specs=[
            pl.BlockSpec((d, tn), lambda i: (0, i)),
            pl.BlockSpec(wqt.shape, lambda i: (0, 0)),
            pl.BlockSpec(keys.shape, lambda i: (0, 0, 0)),
        ],
        out_specs=[route_spec, route_spec, tile_spec],
        out_shape=[route_f32, route_f32, route_b16],
        scratch_shapes=[
            pltpu.VMEM((wqt.shape[0], tn), BF16),
            pltpu.VMEM((2, nk, tn), F32),
            pltpu.VMEM((2, nk, tn), F32),
            pltpu.VMEM((2, nk, tn), F32),
            pltpu.VMEM((2, PEER_TOPK, tn), F32),
        ],
        compiler_params=pltpu.CompilerParams(
            dimension_semantics=("parallel",), vmem_limit_bytes=VMEM_LIMIT),
        name="peer_route",
    )(h1t, wqt, keys)


def _peer_gate_act(row, j, st_ref, at_ref, a_ref, n1_ref, br_ref):
    tm = st_ref.shape[1]
    rows = BF16_ROWS
    a_full = [a_ref[h, row:row + 1, :] for h in range(PEER_HEADS)]
    n_full = [n1_ref[h, row:row + 1, :] for h in range(PEER_HEADS)]
    for lb in range(tm // LANES):
        ls = slice(lb * LANES, (lb + 1) * LANES)
        arow = [jnp.broadcast_to(a_full[h][:, ls], (rows, LANES)).astype(BF16)
                for h in range(PEER_HEADS)]
        nrow = [jnp.broadcast_to(n_full[h][:, ls], (rows, LANES)).astype(BF16)
                for h in range(PEER_HEADS)]
        for rb in range(PEER_NKEYS // rows):
            w = None
            for h in range(PEER_HEADS):
                bb = pltpu.bitcast(br_ref[lb, rb, h, 0], BF16)
                r2 = pltpu.bitcast(br_ref[lb, rb, h, 1], BF16)
                term = arow[h] * jnp.where(r2 < nrow[h], bb, jnp.zeros_like(bb))
                w = term if w is None else w + term
            c0 = j * PEER_NKEYS + rb * rows
            at_ref[c0:c0 + rows, ls] = w * _gelu_exact(st_ref[c0:c0 + rows, ls]).astype(BF16)


def _peer_dense_kernel(xt_ref, u0_ref, ub_ref, un_ref, va_ref, vb_ref,
                       a_ref, n1_ref, br_ref, h1_ref, g2_ref, b2_ref,
                       o_ref, acc_ref, st_a, st_b, at_a, at_b):
    e = pl.program_id(1)
    last = pl.num_programs(1) - 1
    half = PEER_CHUNK // 2

    def scores(u_ref, st_ref):
        st_ref[...] = jnp.dot(u_ref[...], xt_ref[...], preferred_element_type=F32)

    def values(v_ref, at_ref, k):
        ks = slice(k * half, (k + 1) * half)
        acc_ref[...] += jnp.dot(v_ref[:, ks], at_ref[ks, :], preferred_element_type=F32)

    def gate(parity, js, st_ref, at_ref):
        for j in js:
            _peer_gate_act(parity * PEER_SUBS + j, j, st_ref, at_ref, a_ref, n1_ref, br_ref)

    halves = (tuple(range(PEER_SUBS // 2)), tuple(range(PEER_SUBS // 2, PEER_SUBS)))

    @pl.when(e == 0)
    def _():
        acc_ref[...] = jnp.zeros_like(acc_ref)
        scores(u0_ref, st_a)

    scores(ub_ref, st_b)
    for k in range(2):
        gate(0, halves[k], st_a, at_a)
        values(va_ref, at_a, k)
    scores(un_ref, st_a)
    for k in range(2):
        gate(1, halves[k], st_b, at_b)
        values(vb_ref, at_b, k)

    @pl.when(e == last)
    def _():
        r = ALPHA * h1_ref[...] + acc_ref[...].T
        o_ref[...] = _layer_norm(r, g2_ref[...], b2_ref[...])


def _peer_dense(h1t, u_b, vt_b, routes, h1, g2, b2, tm):
    d, t = h1t.shape
    n_exp = u_b.shape[0]
    ch = PEER_CHUNK
    n_ch = n_exp // ch
    assert n_exp % (2 * ch) == 0
    a_t, n1_t, br_t = routes
    once = pl.Buffered(1)
    route_spec = pl.BlockSpec((PEER_HEADS, 2 * PEER_SUBS, tm), lambda i, e: (0, e, i))
    tile_spec = pl.BlockSpec((tm // LANES,) + br_t.shape[1:], lambda i, e: (i, 0, 0, 0, 0, 0),
                             pipeline_mode=once)
    vec = pl.BlockSpec((1, d), lambda i, e: (0, 0))
    return pl.pallas_call(
        _peer_dense_kernel,
        grid=(t // tm, n_ch // 2),
        in_specs=[
            pl.BlockSpec((d, tm), lambda i, e: (0, i), pipeline_mode=once),
            pl.BlockSpec((ch, d), lambda i, e: (0, 0), pipeline_mode=once),
            pl.BlockSpec((ch, d), lambda i, e: (2 * e + 1, 0)),
            pl.BlockSpec((ch, d), lambda i, e: (jnp.minimum(2 * e + 2, n_ch - 1), 0)),
            pl.BlockSpec((d, ch), lambda i, e: (0, 2 * e)),
            pl.BlockSpec((d, ch), lambda i, e: (0, 2 * e + 1)),
            route_spec, route_spec, tile_spec,
            pl.BlockSpec((tm, d), lambda i, e: (i, 0), pipeline_mode=once),
            vec, vec,
        ],
        out_specs=pl.BlockSpec((tm, d), lambda i, e: (i, 0)),
        out_shape=jax.ShapeDtypeStruct((t, d), F32),
        scratch_shapes=[
            pltpu.VMEM((d, tm), F32),
            pltpu.VMEM((ch, tm), F32),
            pltpu.VMEM((ch, tm), F32),
            pltpu.VMEM((ch, tm), BF16),
            pltpu.VMEM((ch, tm), BF16),
        ],
        compiler_params=pltpu.CompilerParams(
            dimension_semantics=("parallel", "arbitrary"), vmem_limit_bytes=VMEM_LIMIT),
        name="peer_dense",
    )(h1t, u_b, u_b, u_b, vt_b, vt_b, a_t, n1_t, br_t, h1, g2, b2)


def _pick(n, prefs):
    for p in prefs:
        if n % p == 0:
            return p
    return n


def kernel(x, meta_tokens, ln_in_g, ln_in_b, w_in, pool_w, pool_scale, conv_w, conv_b, dt_bias,
           A_log, D_skip, ssd_norm_g, w_out, ln1_g, ln1_b, peer_wq, peer_keys, peer_u, peer_v,
           ln2_g, ln2_b):
    bsz, seq, d = x.shape
    assert seq % CHUNK == 0 and w_in.shape[0] == DEPTH
    t = bsz * seq
    n_chunks = seq // CHUNK

    w0 = w_in[0]
    o1, o2, o3 = POOL_WIDTH, POOL_WIDTH + SSD_WIDTH, POOL_WIDTH + SSD_WIDTH + CONV_DIM
    w_main = jnp.concatenate(
        [w0[:, o1:o2], w0[:, o2:o2 + SSD_WIDTH], w0[:, :o1], w0[:, o2 + SSD_WIDTH:o3]],
        axis=1).astype(BF16)
    w_dt = jnp.pad(w0[:, o3:], ((0, 0), (0, LANES - SSD_HEADS))).astype(BF16)
    g_in = ln_in_g.reshape(1, d)
    b_in = ln_in_b.reshape(1, d)
    pad_h = (0, LANES - SSD_HEADS)
    head_of_col = np.arange(SSD_WIDTH) // SSD_HEAD_DIM
    hexp = jnp.asarray((np.arange(LANES)[:, None] == head_of_col[None, :]).astype(np.float32))
    mixer_params = (
        conv_w[0], conv_b[0].reshape(1, CONV_DIM),
        jnp.pad(dt_bias[0], pad_h).reshape(1, LANES), jnp.pad(A_log[0], pad_h).reshape(1, LANES),
        jnp.repeat(D_skip[0], SSD_HEAD_DIM).reshape(1, SSD_WIDTH),
        ssd_norm_g[0].reshape(1, SSD_WIDTH),
        pool_w[0].astype(BF16), pool_scale[0].reshape(1, POOL_WIDTH), hexp)
    w_out_b = w_out[0].astype(BF16)
    wqt = peer_wq[0].T.astype(BF16)
    keys_b = peer_keys[0].astype(BF16)
    u_b = peer_u[0].astype(BF16)
    vt_b = peer_v[0].T.astype(BF16)

    proj_m, dt_m = _ln_inproj(meta_tokens, g_in, b_in, w_main, w_dt, N_META, _pick(PROJ_MAIN, (1024,)))
    n_pad = CHUNK - N_META
    proj_m = jnp.pad(proj_m, ((n_pad, 0), (0, 0)))
    dt_m = jnp.pad(dt_m, ((n_pad, 0), (0, 0)), constant_values=-1e30)
    zero_carries = (jnp.zeros((SSD_GROUPS, SSD_STATE, GROUP_WIDTH), F32),
                    jnp.zeros((CONV_TAIL, CONV_DIM), F32),
                    jnp.zeros((POOL_TAIL, POOL_WIDTH), F32))
    _, st_m, ct_m, pt_m = _mixer(proj_m, dt_m, zero_carries, mixer_params, 1, 1)
    carries = (st_m[0], ct_m[0], pt_m[0])

    x2d = x.reshape(t, d)
    tm1 = _pick(t, (1024, 512, 256, 128))
    proj, dt_raw = _ln_inproj(x2d, g_in, b_in, w_main, w_dt, tm1, _pick(PROJ_MAIN, (1024,)))
    mix, _, _, _ = _mixer(proj, dt_raw, carries, mixer_params, bsz, n_chunks)
    tm3 = _pick(t, (512, 256, 128))
    h1, h1t = _outproj_ln(mix, w_out_b, x2d, g_in, b_in, ln1_g[0].reshape(1, d),
                          ln1_b[0].reshape(1, d), tm3, 1024)
    tn4 = _pick(t, (512, 256, 128))
    routes = _peer_route(h1t, wqt, keys_b, tn4)
    tm5 = _pick(t, (512, 256, 128))
    out = _peer_dense(h1t, u_b, vt_b, routes, h1, ln2_g[0].reshape(1, d), ln2_b[0].reshape(1, d),
                      tm5)
    return out.reshape(bsz, seq, d)
```

```python
import functools
import math

import numpy as np
import jax
import jax.numpy as jnp
from jax import lax
from jax.experimental import pallas as pl
from jax.experimental.pallas import tpu as pltpu

F32 = jnp.float32
BF16 = jnp.bfloat16

N_META = 16
LN_EPS = 1e-5
DEPTH = 1
ALPHA = (2 * DEPTH) ** 0.25
POOL_WINDOWS = (2, 4, 8, 16)
POOL_GROUP_DIM = 256
POOL_WIDTH = 1024
SSD_HEAD_DIM = 64
SSD_WIDTH = 3072
SSD_HEADS = 48
SSD_GROUPS = 8
SSD_STATE = 128
HEADS_PER_GROUP = SSD_HEADS // SSD_GROUPS
GROUP_WIDTH = HEADS_PER_GROUP * SSD_HEAD_DIM
CONV_K = 4
CONV_DIM = SSD_WIDTH + 2 * SSD_GROUPS * SSD_STATE
MIX_WIDTH = POOL_WIDTH + SSD_WIDTH
PROJ_MAIN = POOL_WIDTH + SSD_WIDTH + CONV_DIM
PEER_HEADS = 8
PEER_NKEYS = 128
PEER_TOPK = 16
PEER_HALF = 128
PEER_CHUNK = 512
PEER_SUBS = PEER_CHUNK // PEER_NKEYS

LANES = 128
SUBLANES = 8
BF16_ROWS = 2 * SUBLANES
CHUNK = 128
CONV_TAIL = SUBLANES
POOL_TAIL = 16
VMEM_LIMIT = 60 * 1024 * 1024


def _layer_norm(x, g, b):
    mu = jnp.mean(x, axis=-1, keepdims=True)
    xc = x - mu
    var = jnp.mean(xc * xc, axis=-1, keepdims=True)
    return xc * lax.rsqrt(var + LN_EPS) * g + b


def _silu(x):
    return x * jax.nn.sigmoid(x)


def _softplus(x):
    return jnp.maximum(x, 0.0) + jnp.log1p(jnp.exp(-jnp.abs(x)))


def _gelu_exact(x):
    sqrt_half = np.float32(np.sqrt(0.5))
    return 0.5 * x * (1.0 + lax.erf(x * sqrt_half))


def _ln_inproj_kernel(x_ref, g_ref, b_ref, w_ref, wdt_ref, o_ref, dt_ref, hb_ref):
    @pl.when(pl.program_id(1) == 0)
    def _():
        h = _layer_norm(x_ref[...], g_ref[...], b_ref[...]).astype(BF16)
        hb_ref[...] = h
        dt_ref[...] = jnp.dot(h, wdt_ref[...], preferred_element_type=F32)

    o_ref[...] = jnp.dot(hb_ref[...], w_ref[...], preferred_element_type=F32)


def _ln_inproj(x2d, g, b, w_main, w_dt, tm, tn):
    t, d = x2d.shape
    n = w_main.shape[1]
    return pl.pallas_call(
        _ln_inproj_kernel,
        grid=(t // tm, n // tn),
        in_specs=[
            pl.BlockSpec((tm, d), lambda i, j: (i, 0)),
            pl.BlockSpec((1, d), lambda i, j: (0, 0)),
            pl.BlockSpec((1, d), lambda i, j: (0, 0)),
            pl.BlockSpec((d, tn), lambda i, j: (0, j)),
            pl.BlockSpec((d, LANES), lambda i, j: (0, 0)),
        ],
        out_specs=[
            pl.BlockSpec((tm, tn), lambda i, j: (i, j)),
            pl.BlockSpec((tm, LANES), lambda i, j: (i, 0)),
        ],
        out_shape=[
            jax.ShapeDtypeStruct((t, n), F32),
            jax.ShapeDtypeStruct((t, LANES), F32),
        ],
        scratch_shapes=[pltpu.VMEM((tm, d), BF16)],
        compiler_params=pltpu.CompilerParams(
            dimension_semantics=("parallel", "arbitrary"), vmem_limit_bytes=VMEM_LIMIT),
        name="ln_inproj",
    )(x2d, g, b, w_main, w_dt)


def _mixer_kernel(z_ref, x_ref, p_ref, bm_ref, cm_ref, dt_ref,
                  state0_ref, ctail0_ref, ptail0_ref,
                  convw_ref, convb_ref, dtb_ref, alog_ref, dskip_ref, normg_ref,
                  poolw_ref, pscale_ref, hexp_ref,
                  mix_ref, state_out_ref, ctail_out_ref, ptail_out_ref,
                  cbuf, act, ybuf, pbuf, state):
    q = CHUNK
    c = pl.program_id(1)

    @pl.when(c == 0)
    def _init():
        state[...] = state0_ref[...]
        cbuf[0:CONV_TAIL, :] = ctail0_ref[...]
        pbuf[0:POOL_TAIL, :] = ptail0_ref[...]

    cbuf[CONV_TAIL:CONV_TAIL + q, 0:SSD_WIDTH] = x_ref[...]
    cbuf[CONV_TAIL:CONV_TAIL + q, SSD_WIDTH:SSD_WIDTH + 1024] = bm_ref[...]
    cbuf[CONV_TAIL:CONV_TAIL + q, SSD_WIDTH + 1024:CONV_DIM] = cm_ref[...]
    strip = 512
    for s in range(0, CONV_DIM, strip):
        acc = convb_ref[:, s:s + strip]
        for k in range(CONV_K):
            r0 = CONV_TAIL - (CONV_K - 1) + k
            acc = acc + convw_ref[k:k + 1, s:s + strip] * cbuf[r0:r0 + q, s:s + strip]
        act[:, s:s + strip] = _silu(acc)
    cbuf[0:CONV_TAIL, :] = cbuf[q:q + CONV_TAIL, :]

    dt = _softplus(dt_ref[...] + dtb_ref[...])
    a_neg = -jnp.exp(alog_ref[...])
    adt = dt * a_neg
    row = lax.broadcasted_iota(jnp.int32, (q, q), 0)
    col = lax.broadcasted_iota(jnp.int32, (q, q), 1)
    causal = row >= col
    tril = jnp.where(causal, 1.0, 0.0).astype(F32)
    acs = jnp.dot(tril, adt, preferred_element_type=F32, precision=lax.Precision.HIGHEST)
    aend = acs[q - 1:q, :]
    wmat = jnp.exp(aend - acs) * dt
    acs_t = acs.T
    dt_t = dt.T
    dec_all = jnp.dot(jnp.broadcast_to(jnp.exp(aend), (SUBLANES, LANES)), hexp_ref[...],
                      preferred_element_type=F32, precision=lax.Precision.HIGHEST)[0:1, :]
    lane = lax.broadcasted_iota(jnp.int32, (q, LANES), 1)
    low_half = lane < SSD_HEAD_DIM

    for g in range(SSD_GROUPS):
        bg = act[:, SSD_WIDTH + g * SSD_STATE:SSD_WIDTH + (g + 1) * SSD_STATE]
        cg = act[:, SSD_WIDTH + 1024 + g * SSD_STATE:SSD_WIDTH + 1024 + (g + 1) * SSD_STATE]
        bgb = bg.astype(BF16)
        cb = lax.dot_general(cg.astype(BF16), bgb, (((1,), (1,)), ((), ())),
                             preferred_element_type=F32)
        st_g = state[g]
        st_b = st_g.astype(BF16)
        xw_parts = []
        for pr in range(HEADS_PER_GROUP // 2):
            h0 = g * HEADS_PER_GROUP + 2 * pr
            c0 = h0 * SSD_HEAD_DIM
            xp = act[:, c0:c0 + LANES]
            xpb = xp.astype(BF16)
            stp = st_b[:, pr * LANES:(pr + 1) * LANES]
            ys = []
            for h in (h0, h0 + 1):
                colb = jnp.broadcast_to(acs[:, h:h + 1], (q, q))
                rowb = jnp.broadcast_to(acs_t[h:h + 1, :], (q, q))
                lmat = jnp.exp(jnp.where(causal, colb - rowb, -jnp.inf))
                m = (cb * lmat * jnp.broadcast_to(dt_t[h:h + 1, :], (q, q))).astype(BF16)
                ce = (cg * jnp.exp(colb)).astype(BF16)
                ys.append(jnp.dot(m, xpb, preferred_element_type=F32)
                          + jnp.dot(ce, stp, preferred_element_type=F32))
            ypair = jnp.where(low_half, ys[0], ys[1]) + dskip_ref[:, c0:c0 + LANES] * xp
            ybuf[:, c0:c0 + LANES] = ypair
            wpair = jnp.where(low_half,
                              jnp.broadcast_to(wmat[:, h0:h0 + 1], (q, LANES)),
                              jnp.broadcast_to(wmat[:, h0 + 1:h0 + 2], (q, LANES)))
            xw_parts.append((xp * wpair).astype(BF16))
        xw = jnp.concatenate(xw_parts, axis=1)
        upd = lax.dot_general(bgb, xw, (((0,), (0,)), ((), ())),
                              preferred_element_type=F32)
        state[g] = st_g * dec_all[:, g * GROUP_WIDTH:(g + 1) * GROUP_WIDTH] + upd

    for g in range(SSD_GROUPS):
        cs = slice(g * GROUP_WIDTH, (g + 1) * GROUP_WIDTH)
        yg = ybuf[:, cs] * _silu(z_ref[:, cs])
        ms = jnp.mean(yg * yg, axis=-1, keepdims=True)
        yn = yg * lax.rsqrt(ms + LN_EPS) * normg_ref[:, cs]
        mix_ref[:, POOL_WIDTH + g * GROUP_WIDTH:POOL_WIDTH + (g + 1) * GROUP_WIDTH] = yn.astype(BF16)

    pbuf[POOL_TAIL:POOL_TAIL + q, :] = p_ref[...]
    for gi, w in enumerate(POOL_WINDOWS):
        cs = slice(gi * POOL_GROUP_DIM, (gi + 1) * POOL_GROUP_DIM)
        cur = pbuf[POOL_TAIL:POOL_TAIL + q, cs]
        acc = cur
        for j in range(1, w):
            acc = acc + pbuf[POOL_TAIL - j:POOL_TAIL - j + q, cs]
        pooled = acc * (1.0 / w) - cur
        mixed = jnp.dot(pooled.astype(BF16), poolw_ref[gi], preferred_element_type=F32)
        mix_ref[:, cs] = (mixed * pscale_ref[:, cs]).astype(BF16)
    pbuf[0:POOL_TAIL, :] = pbuf[q:q + POOL_TAIL, :]

    @pl.when(c == pl.num_programs(1) - 1)
    def _fin():
        state_out_ref[0] = state[...]
        ctail_out_ref[0] = cbuf[0:CONV_TAIL, :]
        ptail_out_ref[0] = pbuf[0:POOL_TAIL, :]


def _mixer(proj, dt_raw, carries, params, bsz, n_chunks):
    state0, ctail0, ptail0 = carries
    q = CHUNK
    t = proj.shape[0]

    def tok(col_block):
        return lambda b, c: (b * n_chunks + c, col_block)

    def const(shape):
        nd = len(shape)
        return pl.BlockSpec(shape, lambda b, c: (0,) * nd)

    (convw, convb, dtb, alog, dskip, normg, poolw, pscale, hexp) = params
    return pl.pallas_call(
        _mixer_kernel,
        grid=(bsz, n_chunks),
        in_specs=[
            pl.BlockSpec((q, SSD_WIDTH), tok(0)),
            pl.BlockSpec((q, SSD_WIDTH), tok(1)),
            pl.BlockSpec((q, 1024), tok(6)),
            pl.BlockSpec((q, 1024), tok(7)),
            pl.BlockSpec((q, 1024), tok(8)),
            pl.BlockSpec((q, LANES), tok(0)),
            const(state0.shape), const(ctail0.shape), const(ptail0.shape),
            const(convw.shape), const(convb.shape), const(dtb.shape), const(alog.shape),
            const(dskip.shape), const(normg.shape), const(poolw.shape), const(pscale.shape),
            const(hexp.shape),
        ],
        out_specs=[
            pl.BlockSpec((q, MIX_WIDTH), tok(0)),
            pl.BlockSpec((1,) + state0.shape, lambda b, c: (b, 0, 0, 0)),
            pl.BlockSpec((1,) + ctail0.shape, lambda b, c: (b, 0, 0)),
            pl.BlockSpec((1,) + ptail0.shape, lambda b, c: (b, 0, 0)),
        ],
        out_shape=[
            jax.ShapeDtypeStruct((t, MIX_WIDTH), BF16),
            jax.ShapeDtypeStruct((bsz,) + state0.shape, F32),
            jax.ShapeDtypeStruct((bsz,) + ctail0.shape, F32),
            jax.ShapeDtypeStruct((bsz,) + ptail0.shape, F32),
        ],
        scratch_shapes=[
            pltpu.VMEM((q + CONV_TAIL, CONV_DIM), F32),
            pltpu.VMEM((q, CONV_DIM), F32),
            pltpu.VMEM((q, SSD_WIDTH), F32),
            pltpu.VMEM((q + POOL_TAIL, POOL_WIDTH), F32),
            pltpu.VMEM(state0.shape, F32),
        ],
        compiler_params=pltpu.CompilerParams(
            dimension_semantics=("parallel", "arbitrary"), vmem_limit_bytes=VMEM_LIMIT),
        name="mixer",
    )(proj, proj, proj, proj, proj, dt_raw, state0, ctail0, ptail0,
      convw, convb, dtb, alog, dskip, normg, poolw, pscale, hexp)


def _outproj_ln_kernel(mix_ref, w_ref, x_ref, gin_ref, bin_ref, g1_ref, b1_ref,
                       h1_ref, h1t_ref):
    mixed = jnp.dot(mix_ref[...], w_ref[...], preferred_element_type=F32)
    h0 = _layer_norm(x_ref[...], gin_ref[...], bin_ref[...])
    h1 = _layer_norm(ALPHA * h0 + mixed, g1_ref[...], b1_ref[...])
    h1_ref[...] = h1
    h1t_ref[...] = h1.T.astype(BF16)


def _outproj_ln(mix, w_out, x2d, gin, bin_, g1, b1, tm):
    t, d = x2d.shape
    kdim = mix.shape[1]
    vec = pl.BlockSpec((1, d), lambda i: (0, 0))
    return pl.pallas_call(
        _outproj_ln_kernel,
        grid=(t // tm,),
        in_specs=[
            pl.BlockSpec((tm, kdim), lambda i: (i, 0)),
            pl.BlockSpec((kdim, d), lambda i: (0, 0), pipeline_mode=pl.Buffered(1)),
            pl.BlockSpec((tm, d), lambda i: (i, 0)),
            vec, vec, vec, vec,
        ],
        out_specs=[
            pl.BlockSpec((tm, d), lambda i: (i, 0)),
            pl.BlockSpec((d, tm), lambda i: (0, i)),
        ],
        out_shape=[
            jax.ShapeDtypeStruct((t, d), F32),
            jax.ShapeDtypeStruct((d, t), BF16),
        ],
        compiler_params=pltpu.CompilerParams(
            dimension_semantics=("parallel",), vmem_limit_bytes=VMEM_LIMIT),
        name="outproj_ln",
    )(mix, w_out, x2d, gin, bin_, g1, b1)


_CAND_GROUPS = 10


def _cand_valid_rows(gidx):
    if gidx <= 1:
        return 8
    if gidx == 9:
        return 8
    k1 = gidx - 1
    return PEER_TOPK // (k1 + 1)


def _peer_route_kernel(h1t_ref, wqt_ref, keys_ref, a_ref, n1_ref, br_ref,
                       q_ref, sorig, swork, rank, stop):
    tn = h1t_ref.shape[1]
    nk = PEER_NKEYS
    q_ref[...] = jnp.dot(wqt_ref[...], h1t_ref[...], preferred_element_type=F32).astype(BF16)
    key_iota = lax.broadcasted_iota(jnp.int32, (nk, tn), 0).astype(F32)
    sub_iota = lax.broadcasted_iota(jnp.int32, (SUBLANES, tn), 0)

    def head_body(h, carry):
        for side in range(2):
            r0 = pl.multiple_of((h * 2 + side) * PEER_HALF, PEER_HALF)
            s = jnp.dot(keys_ref[side], q_ref[pl.ds(r0, PEER_HALF), :],
                        preferred_element_type=F32)
            sorig[side] = s
            swork[side] = s
            rank[side] = jnp.full((nk, tn), float(PEER_TOPK), F32)

        def round_body(k, c2):
            kf = lax.convert_element_type(k, F32)
            for side in range(2):
                sw = swork[side]
                m = jnp.max(sw, axis=0, keepdims=True)
                first = jnp.min(jnp.where(sw == m, key_iota, float(nk)), axis=0, keepdims=True)
                onehot = key_iota == first
                rank[side] = jnp.where(onehot, kf, rank[side])
                swork[side] = jnp.where(onehot, -jnp.inf, sw)
                stop[side, pl.ds(k, 1), :] = m
            return c2

        lax.fori_loop(0, PEER_TOPK, round_body, 0)

        st1 = stop[0]
        st2 = stop[1]
        top1 = st1[0:1, :]
        top2 = st2[0:1, :]
        cmax = top1 + top2

        def bc(rowv):
            return jnp.broadcast_to(rowv, (SUBLANES, tn))

        groups = [st2[0:8, :] + bc(top1), st2[8:16, :] + bc(top1)]
        for k1 in range(1, 8):
            groups.append(st2[0:8, :] + bc(st1[k1:k1 + 1, :]))
        groups.append(st1[8:16, :] + bc(top2))

        counts = [jnp.zeros((SUBLANES, tn), F32) for _ in range(_CAND_GROUPS)]
        for gp in range(_CAND_GROUPS):
            for rp in range(_cand_valid_rows(gp)):
                cp = bc(groups[gp][rp:rp + 1, :])
                for gq in range(_CAND_GROUPS):
                    if gq > gp:
                        beats = cp >= groups[gq]
                    elif gq < gp:
                        beats = cp > groups[gq]
                    else:
                        beats = jnp.logical_or(
                            cp > groups[gq],
                            jnp.logical_and(cp == groups[gq], sub_iota > rp))
                    counts[gq] = counts[gq] + jnp.where(beats, 1.0, 0.0)
        sel = []
        zsum = jnp.zeros((SUBLANES, tn), F32)
        for gq in range(_CAND_GROUPS):
            ok = jnp.logical_and(counts[gq] < float(PEER_TOPK), sub_iota < _cand_valid_rows(gq))
            sg = jnp.where(ok, 1.0, 0.0)
            sel.append(sg)
            zsum = zsum + sg * jnp.exp(groups[gq] - bc(cmax))
        inv_z = 1.0 / jnp.sum(zsum, axis=0, keepdims=True)

        n_rows = [jnp.sum(sel[0] + sel[1], axis=0, keepdims=True)]
        for k1 in range(1, 8):
            n_rows.append(jnp.sum(sel[k1 + 1], axis=0, keepdims=True))
        for k1 in range(8, 16):
            n_rows.append(sel[9][k1 - 8:k1 - 7, :])

        r1 = rank[0]
        n1 = jnp.zeros((nk, tn), F32)
        for k1 in range(PEER_TOPK):
            n1 = jnp.where(r1 == float(k1), jnp.broadcast_to(n_rows[k1], (nk, tn)), n1)

        a_ref[h] = jnp.exp(sorig[0] - top1) * inv_z
        n1_ref[h] = n1
        packed = (pltpu.bitcast(jnp.exp(sorig[1] - top2).astype(BF16), jnp.uint32),
                  pltpu.bitcast(rank[1].astype(BF16), jnp.uint32))
        for which in range(2):
            for lb in range(tn // LANES):
                for rb in range(nk // BF16_ROWS):
                    br_ref[lb, rb, h, which] = packed[which][
                        rb * SUBLANES:(rb + 1) * SUBLANES, lb * LANES:(lb + 1) * LANES]
        return carry

    lax.fori_loop(0, PEER_HEADS, head_body, 0)


def _peer_route(h1t, wqt, keys, tn):
    d, t = h1t.shape
    nk = PEER_NKEYS
    route_spec = pl.BlockSpec((PEER_HEADS, nk, tn), lambda i: (0, 0, i))
    route_f32 = jax.ShapeDtypeStruct((PEER_HEADS, nk, t), F32)
    tile_dims = (nk // BF16_ROWS, PEER_HEADS, 2, SUBLANES, LANES)
    tile_spec = pl.BlockSpec((tn // LANES,) + tile_dims, lambda i: (i, 0, 0, 0, 0, 0))
    route_b16 = jax.ShapeDtypeStruct((t // LANES,) + tile_dims, jnp.uint32)
    return pl.pallas_call(
        _peer_route_kernel,
        grid=(t // tn,),
        in_specs=[
            pl.BlockSpec((d, tn), lambda i: (0, i)),
            pl.BlockSpec(wqt.shape, lambda i: (0, 0)),
            pl.BlockSpec(keys.shape, lambda i: (0, 0, 0)),
        ],
        out_specs=[route_spec, route_spec, tile_spec],
        out_shape=[route_f32, route_f32, route_b16],
        scratch_shapes=[
            pltpu.VMEM((wqt.shape[0], tn), BF16),
            pltpu.VMEM((2, nk, tn), F32),
            pltpu.VMEM((2, nk, tn), F32),
            pltpu.VMEM((2, nk, tn), F32),
            pltpu.VMEM((2, PEER_TOPK, tn), F32),
        ],
        compiler_params=pltpu.CompilerParams(
            dimension_semantics=("parallel",), vmem_limit_bytes=VMEM_LIMIT),
        name="peer_route",
    )(h1t, wqt, keys)


def _peer_gate_act(row, j, st_ref, at_ref, a_ref, n1_ref, br_ref):
    tm = st_ref.shape[1]
    rows = BF16_ROWS
    a_full = [a_ref[h, row:row + 1, :] for h in range(PEER_HEADS)]
    n_full = [n1_ref[h, row:row + 1, :] for h in range(PEER_HEADS)]
    for lb in range(tm // LANES):
        ls = slice(lb * LANES, (lb + 1) * LANES)
        arow = [jnp.broadcast_to(a_full[h][:, ls], (rows, LANES)).astype(BF16)
                for h in range(PEER_HEADS)]
        nrow = [jnp.broadcast_to(n_full[h][:, ls], (rows, LANES)).astype(BF16)
                for h in range(PEER_HEADS)]
        for rb in range(PEER_NKEYS // rows):
            w = None
            for h in range(PEER_HEADS):
                bb = pltpu.bitcast(br_ref[lb, rb, h, 0], BF16)
                r2 = pltpu.bitcast(br_ref[lb, rb, h, 1], BF16)
                term = arow[h] * jnp.where(r2 < nrow[h], bb, jnp.zeros_like(bb))
                w = term if w is None else w + term
            c0 = j * PEER_NKEYS + rb * rows
            at_ref[c0:c0 + rows, ls] = w * _gelu_exact(st_ref[c0:c0 + rows, ls]).astype(BF16)


def _peer_dense_kernel(xt_ref, u0_ref, ub_ref, un_ref, va_ref, vb_ref,
                       a_ref, n1_ref, br_ref, acc_ref, st_a, st_b, at_a, at_b):
    e = pl.program_id(1)
    half = PEER_CHUNK // 2

    def scores(u_ref, st_ref):
        st_ref[...] = jnp.dot(u_ref[...], xt_ref[...], preferred_element_type=F32)

    def values(v_ref, at_ref, k):
        ks = slice(k * half, (k + 1) * half)
        acc_ref[...] += jnp.dot(v_ref[:, ks], at_ref[ks, :], preferred_element_type=F32)

    def gate(parity, js, st_ref, at_ref):
        for j in js:
            _peer_gate_act(parity * PEER_SUBS + j, j, st_ref, at_ref, a_ref, n1_ref, br_ref)

    halves = (tuple(range(PEER_SUBS // 2)), tuple(range(PEER_SUBS // 2, PEER_SUBS)))

    @pl.when(e == 0)
    def _():
        acc_ref[...] = jnp.zeros_like(acc_ref)
        scores(u0_ref, st_a)

    scores(ub_ref, st_b)
    for k in range(2):
        gate(0, halves[k], st_a, at_a)
        values(va_ref, at_a, k)
    scores(un_ref, st_a)
    for k in range(2):
        gate(1, halves[k], st_b, at_b)
        values(vb_ref, at_b, k)


def _peer_dense(h1t, u_b, vt_b, routes, tm):
    d, t = h1t.shape
    n_exp = u_b.shape[0]
    ch = PEER_CHUNK
    n_ch = n_exp // ch
    assert n_exp % (2 * ch) == 0
    a_t, n1_t, br_t = routes
    once = pl.Buffered(1)
    route_spec = pl.BlockSpec((PEER_HEADS, 2 * PEER_SUBS, tm), lambda i, e: (0, e, i))
    tile_spec = pl.BlockSpec((tm // LANES,) + br_t.shape[1:], lambda i, e: (i, 0, 0, 0, 0, 0),
                             pipeline_mode=once)
    vec = pl.BlockSpec((1, d), lambda i, e: (0, 0))
    return pl.pallas_call(
        _peer_dense_kernel,
        grid=(t // tm, n_ch // 2),
        in_specs=[
            pl.BlockSpec((d, tm), lambda i, e: (0, i), pipeline_mode=once),
            pl.BlockSpec((ch, d), lambda i, e: (0, 0), pipeline_mode=once),
            pl.BlockSpec((ch, d), lambda i, e: (2 * e + 1, 0)),
            pl.BlockSpec((ch, d), lambda i, e: (jnp.minimum(2 * e + 2, n_ch - 1), 0)),
            pl.BlockSpec((d, ch), lambda i, e: (0, 2 * e)),
            pl.BlockSpec((d, ch), lambda i, e: (0, 2 * e + 1)),
            route_spec, route_spec, tile_spec,
        ],
        out_specs=pl.BlockSpec((d, tm), lambda i, e: (0, i)),
        out_shape=jax.ShapeDtypeStruct((d, t), F32),
        scratch_shapes=[
            pltpu.VMEM((ch, tm), F32),
            pltpu.VMEM((ch, tm), F32),
            pltpu.VMEM((ch, tm), BF16),
            pltpu.VMEM((ch, tm), BF16),
        ],
        compiler_params=pltpu.CompilerParams(
            dimension_semantics=("parallel", "arbitrary"), vmem_limit_bytes=VMEM_LIMIT),
        name="peer_dense",
    )(h1t, u_b, u_b, u_b, vt_b, vt_b, a_t, n1_t, br_t)


def _peer_out_kernel(pt_ref, h1_ref, g_ref, b_ref, o_ref):
    r = ALPHA * h1_ref[...] + pt_ref[...].T
    o_ref[...] = _layer_norm(r, g_ref[...], b_ref[...])


def _peer_out(peer_t, h1, g2, b2, tm):
    t, d = h1.shape
    vec = pl.BlockSpec((1, d), lambda i: (0, 0))
    return pl.pallas_call(
        _peer_out_kernel,
        grid=(t // tm,),
        in_specs=[pl.BlockSpec((d, tm), lambda i: (0, i)),
                  pl.BlockSpec((tm, d), lambda i: (i, 0)), vec, vec],
        out_specs=pl.BlockSpec((tm, d), lambda i: (i, 0)),
        out_shape=jax.ShapeDtypeStruct((t, d), F32),
        compiler_params=pltpu.CompilerParams(
            dimension_semantics=("parallel",), vmem_limit_bytes=VMEM_LIMIT),
        name="peer_out",
    )(peer_t, h1, g2, b2)


def _pick(n, prefs):
    for p in prefs:
        if n % p == 0:
            return p
    return n


def kernel(x, meta_tokens, ln_in_g, ln_in_b, w_in, pool_w, pool_scale, conv_w, conv_b, dt_bias,
           A_log, D_skip, ssd_norm_g, w_out, ln1_g, ln1_b, peer_wq, peer_keys, peer_u, peer_v,
           ln2_g, ln2_b):
    bsz, seq, d = x.shape
    assert seq % CHUNK == 0 and w_in.shape[0] == DEPTH
    t = bsz * seq
    n_chunks = seq // CHUNK

    w0 = w_in[0]
    o1, o2, o3 = POOL_WIDTH, POOL_WIDTH + SSD_WIDTH, POOL_WIDTH + SSD_WIDTH + CONV_DIM
    w_main = jnp.concatenate(
        [w0[:, o1:o2], w0[:, o2:o2 + SSD_WIDTH], w0[:, :o1], w0[:, o2 + SSD_WIDTH:o3]],
        axis=1).astype(BF16)
    w_dt = jnp.pad(w0[:, o3:], ((0, 0), (0, LANES - SSD_HEADS))).astype(BF16)
    g_in = ln_in_g.reshape(1, d)
    b_in = ln_in_b.reshape(1, d)
    pad_h = (0, LANES - SSD_HEADS)
    head_of_col = np.arange(SSD_WIDTH) // SSD_HEAD_DIM
    hexp = jnp.asarray((np.arange(LANES)[:, None] == head_of_col[None, :]).astype(np.float32))
    mixer_params = (
        conv_w[0], conv_b[0].reshape(1, CONV_DIM),
        jnp.pad(dt_bias[0], pad_h).reshape(1, LANES), jnp.pad(A_log[0], pad_h).reshape(1, LANES),
        jnp.repeat(D_skip[0], SSD_HEAD_DIM).reshape(1, SSD_WIDTH),
        ssd_norm_g[0].reshape(1, SSD_WIDTH),
        pool_w[0].astype(BF16), pool_scale[0].reshape(1, POOL_WIDTH), hexp)
    w_out_b = w_out[0].astype(BF16)
    wqt = peer_wq[0].T.astype(BF16)
    keys_b = peer_keys[0].astype(BF16)
    u_b = peer_u[0].astype(BF16)
    vt_b = peer_v[0].T.astype(BF16)

    proj_m, dt_m = _ln_inproj(meta_tokens, g_in, b_in, w_main, w_dt, N_META, _pick(PROJ_MAIN, (1024,)))
    n_pad = CHUNK - N_META
    proj_m = jnp.pad(proj_m, ((n_pad, 0), (0, 0)))
    dt_m = jnp.pad(dt_m, ((n_pad, 0), (0, 0)), constant_values=-1e30)
    zero_carries = (jnp.zeros((SSD_GROUPS, SSD_STATE, GROUP_WIDTH), F32),
                    jnp.zeros((CONV_TAIL, CONV_DIM), F32),
                    jnp.zeros((POOL_TAIL, POOL_WIDTH), F32))
    _, st_m, ct_m, pt_m = _mixer(proj_m, dt_m, zero_carries, mixer_params, 1, 1)
    carries = (st_m[0], ct_m[0], pt_m[0])

    x2d = x.reshape(t, d)
    tm1 = _pick(t, (1024, 512, 256, 128))
    proj, dt_raw = _ln_inproj(x2d, g_in, b_in, w_main, w_dt, tm1, _pick(PROJ_MAIN, (1024,)))
    mix, _, _, _ = _mixer(proj, dt_raw, carries, mixer_params, bsz, n_chunks)
    tm3 = _pick(t, (512, 256, 128))
    h1, h1t = _outproj_ln(mix, w_out_b, x2d, g_in, b_in, ln1_g[0].reshape(1, d),
                          ln1_b[0].reshape(1, d), tm3)
    tn4 = _pick(t, (512, 256, 128))
    routes = _peer_route(h1t, wqt, keys_b, tn4)
    tm5 = _pick(t, (1024, 512, 256, 128))
    peer_t = _peer_dense(h1t, u_b, vt_b, routes, tm5)
    out = _peer_out(peer_t, h1, ln2_g[0].reshape(1, d), ln2_b[0].reshape(1, d),
                    _pick(t, (512, 256, 128)))
    return out.reshape(bsz, seq, d)
```

```python
import functools
import math

import numpy as np
import jax
import jax.numpy as jnp
from jax import lax
from jax.experimental import pallas as pl
from jax.experimental.pallas import tpu as pltpu

F32 = jnp.float32
BF16 = jnp.bfloat16

N_META = 16
LN_EPS = 1e-5
DEPTH = 1
ALPHA = (2 * DEPTH) ** 0.25
POOL_WINDOWS = (2, 4, 8, 16)
POOL_GROUP_DIM = 256
POOL_WIDTH = 1024
SSD_HEAD_DIM = 64
SSD_WIDTH = 3072
SSD_HEADS = 48
SSD_GROUPS = 8
SSD_STATE = 128
HEADS_PER_GROUP = SSD_HEADS // SSD_GROUPS
GROUP_WIDTH = HEADS_PER_GROUP * SSD_HEAD_DIM
CONV_K = 4
CONV_DIM = SSD_WIDTH + 2 * SSD_GROUPS * SSD_STATE
MIX_WIDTH = POOL_WIDTH + SSD_WIDTH
PROJ_MAIN = POOL_WIDTH + SSD_WIDTH + CONV_DIM
PEER_HEADS = 8
PEER_NKEYS = 128
PEER_TOPK = 16
PEER_HALF = 128
PEER_CHUNK = 512
PEER_SUBS = PEER_CHUNK // PEER_NKEYS

LANES = 128
SUBLANES = 8
BF16_ROWS = 2 * SUBLANES
CHUNK = 128
CONV_TAIL = SUBLANES
POOL_TAIL = 16
VMEM_LIMIT = 60 * 1024 * 1024


def _layer_norm(x, g, b):
    mu = jnp.mean(x, axis=-1, keepdims=True)
    xc = x - mu
    var = jnp.mean(xc * xc, axis=-1, keepdims=True)
    return xc * lax.rsqrt(var + LN_EPS) * g + b


def _silu(x):
    return x * jax.nn.sigmoid(x)


def _softplus(x):
    return jnp.maximum(x, 0.0) + jnp.log1p(jnp.exp(-jnp.abs(x)))


def _gelu_exact(x):
    sqrt_half = np.float32(np.sqrt(0.5))
    return 0.5 * x * (1.0 + lax.erf(x * sqrt_half))


def _ln_inproj_kernel(x_ref, g_ref, b_ref, w_ref, wdt_ref, o_ref, dt_ref, hb_ref):
    @pl.when(pl.program_id(1) == 0)
    def _():
        h = _layer_norm(x_ref[...], g_ref[...], b_ref[...]).astype(BF16)
        hb_ref[...] = h
        dt_ref[...] = jnp.dot(h, wdt_ref[...], preferred_element_type=F32)

    o_ref[...] = jnp.dot(hb_ref[...], w_ref[...], preferred_element_type=F32)


def _ln_inproj(x2d, g, b, w_main, w_dt, tm, tn):
    t, d = x2d.shape
    n = w_main.shape[1]
    return pl.pallas_call(
        _ln_inproj_kernel,
        grid=(t // tm, n // tn),
        in_specs=[
            pl.BlockSpec((tm, d), lambda i, j: (i, 0)),
            pl.BlockSpec((1, d), lambda i, j: (0, 0)),
            pl.BlockSpec((1, d), lambda i, j: (0, 0)),
            pl.BlockSpec((d, tn), lambda i, j: (0, j)),
            pl.BlockSpec((d, LANES), lambda i, j: (0, 0)),
        ],
        out_specs=[
            pl.BlockSpec((tm, tn), lambda i, j: (i, j)),
            pl.BlockSpec((tm, LANES), lambda i, j: (i, 0)),
        ],
        out_shape=[
            jax.ShapeDtypeStruct((t, n), F32),
            jax.ShapeDtypeStruct((t, LANES), F32),
        ],
        scratch_shapes=[pltpu.VMEM((tm, d), BF16)],
        compiler_params=pltpu.CompilerParams(
            dimension_semantics=("parallel", "arbitrary"), vmem_limit_bytes=VMEM_LIMIT),
        name="ln_inproj",
    )(x2d, g, b, w_main, w_dt)


def _mixer_kernel(z_ref, x_ref, p_ref, bm_ref, cm_ref, dt_ref,
                  state0_ref, ctail0_ref, ptail0_ref,
                  convw_ref, convb_ref, dtb_ref, alog_ref, dskip_ref, normg_ref,
                  poolw_ref, pscale_ref, hexp_ref,
                  mix_ref, state_out_ref, ctail_out_ref, ptail_out_ref,
                  cbuf, act, ybuf, pbuf, state):
    q = CHUNK
    c = pl.program_id(1)

    @pl.when(c == 0)
    def _init():
        state[...] = state0_ref[...]
        cbuf[0:CONV_TAIL, :] = ctail0_ref[...]
        pbuf[0:POOL_TAIL, :] = ptail0_ref[...]

    cbuf[CONV_TAIL:CONV_TAIL + q, 0:SSD_WIDTH] = x_ref[...]
    cbuf[CONV_TAIL:CONV_TAIL + q, SSD_WIDTH:SSD_WIDTH + 1024] = bm_ref[...]
    cbuf[CONV_TAIL:CONV_TAIL + q, SSD_WIDTH + 1024:CONV_DIM] = cm_ref[...]
    strip = 512
    for s in range(0, CONV_DIM, strip):
        acc = convb_ref[:, s:s + strip]
        for k in range(CONV_K):
            r0 = CONV_TAIL - (CONV_K - 1) + k
            acc = acc + convw_ref[k:k + 1, s:s + strip] * cbuf[r0:r0 + q, s:s + strip]
        act[:, s:s + strip] = _silu(acc)
    cbuf[0:CONV_TAIL, :] = cbuf[q:q + CONV_TAIL, :]

    dt = _softplus(dt_ref[...] + dtb_ref[...])
    a_neg = -jnp.exp(alog_ref[...])
    adt = dt * a_neg
    row = lax.broadcasted_iota(jnp.int32, (q, q), 0)
    col = lax.broadcasted_iota(jnp.int32, (q, q), 1)
    causal = row >= col
    tril = jnp.where(causal, 1.0, 0.0).astype(F32)
    acs = jnp.dot(tril, adt, preferred_element_type=F32, precision=lax.Precision.HIGHEST)
    aend = acs[q - 1:q, :]
    wmat = jnp.exp(aend - acs) * dt
    acs_t = acs.T
    dt_t = dt.T
    dec_all = jnp.dot(jnp.broadcast_to(jnp.exp(aend), (SUBLANES, LANES)), hexp_ref[...],
                      preferred_element_type=F32, precision=lax.Precision.HIGHEST)[0:1, :]
    lane = lax.broadcasted_iota(jnp.int32, (q, LANES), 1)
    low_half = lane < SSD_HEAD_DIM

    for g in range(SSD_GROUPS):
        bg = act[:, SSD_WIDTH + g * SSD_STATE:SSD_WIDTH + (g + 1) * SSD_STATE]
        cg = act[:, SSD_WIDTH + 1024 + g * SSD_STATE:SSD_WIDTH + 1024 + (g + 1) * SSD_STATE]
        bgb = bg.astype(BF16)
        cb = lax.dot_general(cg.astype(BF16), bgb, (((1,), (1,)), ((), ())),
                             preferred_element_type=F32)
        st_g = state[g]
        st_b = st_g.astype(BF16)
        xw_parts = []
        for pr in range(HEADS_PER_GROUP // 2):
            h0 = g * HEADS_PER_GROUP + 2 * pr
            c0 = h0 * SSD_HEAD_DIM
            xp = act[:, c0:c0 + LANES]
            xpb = xp.astype(BF16)
            stp = st_b[:, pr * LANES:(pr + 1) * LANES]
            ys = []
            for h in (h0, h0 + 1):
                colb = jnp.broadcast_to(acs[:, h:h + 1], (q, q))
                rowb = jnp.broadcast_to(acs_t[h:h + 1, :], (q, q))
                lmat = jnp.exp(jnp.where(causal, colb - rowb, -jnp.inf))
                m = (cb * lmat * jnp.broadcast_to(dt_t[h:h + 1, :], (q, q))).astype(BF16)
                ce = (cg * jnp.exp(colb)).astype(BF16)
                ys.append(jnp.dot(m, xpb, preferred_element_type=F32)
                          + jnp.dot(ce, stp, preferred_element_type=F32))
            ypair = jnp.where(low_half, ys[0], ys[1]) + dskip_ref[:, c0:c0 + LANES] * xp
            ybuf[:, c0:c0 + LANES] = ypair
            wpair = jnp.where(low_half,
                              jnp.broadcast_to(wmat[:, h0:h0 + 1], (q, LANES)),
                              jnp.broadcast_to(wmat[:, h0 + 1:h0 + 2], (q, LANES)))
            xw_parts.append((xp * wpair).astype(BF16))
        xw = jnp.concatenate(xw_parts, axis=1)
        upd = lax.dot_general(bgb, xw, (((0,), (0,)), ((), ())),
                              preferred_element_type=F32)
        state[g] = st_g * dec_all[:, g * GROUP_WIDTH:(g + 1) * GROUP_WIDTH] + upd

    for g in range(SSD_GROUPS):
        cs = slice(g * GROUP_WIDTH, (g + 1) * GROUP_WIDTH)
        yg = ybuf[:, cs] * _silu(z_ref[:, cs])
        ms = jnp.mean(yg * yg, axis=-1, keepdims=True)
        yn = yg * lax.rsqrt(ms + LN_EPS) * normg_ref[:, cs]
        mix_ref[:, POOL_WIDTH + g * GROUP_WIDTH:POOL_WIDTH + (g + 1) * GROUP_WIDTH] = yn.astype(BF16)

    pbuf[POOL_TAIL:POOL_TAIL + q, :] = p_ref[...]
    for gi, w in enumerate(POOL_WINDOWS):
        cs = slice(gi * POOL_GROUP_DIM, (gi + 1) * POOL_GROUP_DIM)
        cur = pbuf[POOL_TAIL:POOL_TAIL + q, cs]
        acc = cur
        for j in range(1, w):
            acc = acc + pbuf[POOL_TAIL - j:POOL_TAIL - j + q, cs]
        pooled = acc * (1.0 / w) - cur
        mixed = jnp.dot(pooled.astype(BF16), poolw_ref[gi], preferred_element_type=F32)
        mix_ref[:, cs] = (mixed * pscale_ref[:, cs]).astype(BF16)
    pbuf[0:POOL_TAIL, :] = pbuf[q:q + POOL_TAIL, :]

    @pl.when(c == pl.num_programs(1) - 1)
    def _fin():
        state_out_ref[0] = state[...]
        ctail_out_ref[0] = cbuf[0:CONV_TAIL, :]
        ptail_out_ref[0] = pbuf[0:POOL_TAIL, :]


def _mixer(proj, dt_raw, carries, params, bsz, n_chunks):
    state0, ctail0, ptail0 = carries
    q = CHUNK
    t = proj.shape[0]

    def tok(col_block):
        return lambda b, c: (b * n_chunks + c, col_block)

    def const(shape):
        nd = len(shape)
        return pl.BlockSpec(shape, lambda b, c: (0,) * nd)

    (convw, convb, dtb, alog, dskip, normg, poolw, pscale, hexp) = params
    return pl.pallas_call(
        _mixer_kernel,
        grid=(bsz, n_chunks),
        in_specs=[
            pl.BlockSpec((q, SSD_WIDTH), tok(0)),
            pl.BlockSpec((q, SSD_WIDTH), tok(1)),
            pl.BlockSpec((q, 1024), tok(6)),
            pl.BlockSpec((q, 1024), tok(7)),
            pl.BlockSpec((q, 1024), tok(8)),
            pl.BlockSpec((q, LANES), tok(0)),
            const(state0.shape), const(ctail0.shape), const(ptail0.shape),
            const(convw.shape), const(convb.shape), const(dtb.shape), const(alog.shape),
            const(dskip.shape), const(normg.shape), const(poolw.shape), const(pscale.shape),
            const(hexp.shape),
        ],
        out_specs=[
            pl.BlockSpec((q, MIX_WIDTH), tok(0)),
            pl.BlockSpec((1,) + state0.shape, lambda b, c: (b, 0, 0, 0)),
            pl.BlockSpec((1,) + ctail0.shape, lambda b, c: (b, 0, 0)),
            pl.BlockSpec((1,) + ptail0.shape, lambda b, c: (b, 0, 0)),
        ],
        out_shape=[
            jax.ShapeDtypeStruct((t, MIX_WIDTH), BF16),
            jax.ShapeDtypeStruct((bsz,) + state0.shape, F32),
            jax.ShapeDtypeStruct((bsz,) + ctail0.shape, F32),
            jax.ShapeDtypeStruct((bsz,) + ptail0.shape, F32),
        ],
        scratch_shapes=[
            pltpu.VMEM((q + CONV_TAIL, CONV_DIM), F32),
            pltpu.VMEM((q, CONV_DIM), F32),
            pltpu.VMEM((q, SSD_WIDTH), F32),
            pltpu.VMEM((q + POOL_TAIL, POOL_WIDTH), F32),
            pltpu.VMEM(state0.shape, F32),
        ],
        compiler_params=pltpu.CompilerParams(
            dimension_semantics=("parallel", "arbitrary"), vmem_limit_bytes=VMEM_LIMIT),
        name="mixer",
    )(proj, proj, proj, proj, proj, dt_raw, state0, ctail0, ptail0,
      convw, convb, dtb, alog, dskip, normg, poolw, pscale, hexp)


def _outproj_ln_kernel(mix_ref, w_ref, x_ref, gin_ref, bin_ref, g1_ref, b1_ref,
                       h1_ref, h1t_ref):
    mixed = jnp.dot(mix_ref[...], w_ref[...], preferred_element_type=F32)
    h0 = _layer_norm(x_ref[...], gin_ref[...], bin_ref[...])
    h1 = _layer_norm(ALPHA * h0 + mixed, g1_ref[...], b1_ref[...])
    h1_ref[...] = h1
    h1t_ref[...] = h1.T.astype(BF16)


def _outproj_ln(mix, w_out, x2d, gin, bin_, g1, b1, tm):
    t, d = x2d.shape
    kdim = mix.shape[1]
    vec = pl.BlockSpec((1, d), lambda i: (0, 0))
    return pl.pallas_call(
        _outproj_ln_kernel,
        grid=(t // tm,),
        in_specs=[
            pl.BlockSpec((tm, kdim), lambda i: (i, 0)),
            pl.BlockSpec((kdim, d), lambda i: (0, 0), pipeline_mode=pl.Buffered(1)),
            pl.BlockSpec((tm, d), lambda i: (i, 0)),
            vec, vec, vec, vec,
        ],
        out_specs=[
            pl.BlockSpec((tm, d), lambda i: (i, 0)),
            pl.BlockSpec((d, tm), lambda i: (0, i)),
        ],
        out_shape=[
            jax.ShapeDtypeStruct((t, d), F32),
            jax.ShapeDtypeStruct((d, t), BF16),
        ],
        compiler_params=pltpu.CompilerParams(
            dimension_semantics=("parallel",), vmem_limit_bytes=VMEM_LIMIT),
        name="outproj_ln",
    )(mix, w_out, x2d, gin, bin_, g1, b1)


_CAND_GROUPS = 10


def _cand_valid_rows(gidx):
    if gidx <= 1:
        return 8
    if gidx == 9:
        return 8
    k1 = gidx - 1
    return PEER_TOPK // (k1 + 1)


def _peer_route_kernel(h1t_ref, wqt_ref, keys_ref, a_ref, n1_ref, br_ref,
                       q_ref, sorig, swork, rank, stop):
    tn = h1t_ref.shape[1]
    nk = PEER_NKEYS
    q_ref[...] = jnp.dot(wqt_ref[...], h1t_ref[...], preferred_element_type=F32).astype(BF16)
    key_iota = lax.broadcasted_iota(jnp.int32, (nk, tn), 0).astype(F32)
    sub_iota = lax.broadcasted_iota(jnp.int32, (SUBLANES, tn), 0)

    def head_body(h, carry):
        for side in range(2):
            r0 = pl.multiple_of((h * 2 + side) * PEER_HALF, PEER_HALF)
            s = jnp.dot(keys_ref[side], q_ref[pl.ds(r0, PEER_HALF), :],
                        preferred_element_type=F32)
            sorig[side] = s

        def extract_distinct():
            for side in range(2):
                rank[side] = jnp.full((nk, tn), -1.0, F32)

            def round_body(k, prev):
                nxt = []
                for side in range(2):
                    s = sorig[side]
                    below = s < prev[side]
                    m = jnp.max(jnp.where(below, s, -jnp.inf), axis=0, keepdims=True)
                    rank[side] = rank[side] + jnp.where(below, 1.0, 0.0)
                    stop[side, pl.ds(k, 1), :] = m
                    nxt.append(m)
                return tuple(nxt)

            inf_row = jnp.full((1, tn), jnp.inf, F32)
            last = lax.fori_loop(0, PEER_TOPK, round_body, (inf_row, inf_row))
            for side in range(2):
                rank[side] = rank[side] + jnp.where(sorig[side] < last[side], 1.0, 0.0)

        def extract_tie_broken():
            for side in range(2):
                swork[side] = sorig[side]
                rank[side] = jnp.full((nk, tn), float(PEER_TOPK), F32)

            def round_body(k, c2):
                kf = lax.convert_element_type(k, F32)
                for side in range(2):
                    sw = swork[side]
                    m = jnp.max(sw, axis=0, keepdims=True)
                    first = jnp.min(jnp.where(sw == m, key_iota, float(nk)), axis=0, keepdims=True)
                    hit = key_iota == first
                    rank[side] = jnp.where(hit, kf, rank[side])
                    swork[side] = jnp.where(hit, -jnp.inf, sw)
                    stop[side, pl.ds(k, 1), :] = m
                return c2

            lax.fori_loop(0, PEER_TOPK, round_body, 0)

        extract_distinct()
        excess = jnp.zeros((1, tn), F32)
        for side in range(2):
            ranked = jnp.sum(jnp.where(rank[side] < float(PEER_TOPK), 1.0, 0.0), axis=0, keepdims=True)
            excess = excess + jnp.abs(ranked - float(PEER_TOPK))

        @pl.when(jnp.max(excess) > 0.0)
        def _():
            extract_tie_broken()

        st1 = stop[0]
        st2 = stop[1]
        top1 = st1[0:1, :]
        top2 = st2[0:1, :]
        cmax = top1 + top2

        def bc(rowv):
            return jnp.broadcast_to(rowv, (SUBLANES, tn))

        groups = [st2[0:8, :] + bc(top1), st2[8:16, :] + bc(top1)]
        for k1 in range(1, 8):
            groups.append(st2[0:8, :] + bc(st1[k1:k1 + 1, :]))
        groups.append(st1[8:16, :] + bc(top2))

        counts = [jnp.zeros((SUBLANES, tn), F32) for _ in range(_CAND_GROUPS)]
        for gp in range(_CAND_GROUPS):
            for rp in range(_cand_valid_rows(gp)):
                cp = bc(groups[gp][rp:rp + 1, :])
                for gq in range(_CAND_GROUPS):
                    if gq > gp:
                        beats = cp >= groups[gq]
                    elif gq < gp:
                        beats = cp > groups[gq]
                    else:
                        beats = jnp.logical_or(
                            cp > groups[gq],
                            jnp.logical_and(cp == groups[gq], sub_iota > rp))
                    counts[gq] = counts[gq] + jnp.where(beats, 1.0, 0.0)
        sel = []
        zsum = jnp.zeros((SUBLANES, tn), F32)
        for gq in range(_CAND_GROUPS):
            ok = jnp.logical_and(counts[gq] < float(PEER_TOPK), sub_iota < _cand_valid_rows(gq))
            sg = jnp.where(ok, 1.0, 0.0)
            sel.append(sg)
            zsum = zsum + sg * jnp.exp(groups[gq] - bc(cmax))
        inv_z = 1.0 / jnp.sum(zsum, axis=0, keepdims=True)

        n_rows = [jnp.sum(sel[0] + sel[1], axis=0, keepdims=True)]
        for k1 in range(1, 8):
            n_rows.append(jnp.sum(sel[k1 + 1], axis=0, keepdims=True))
        for k1 in range(8, 16):
            n_rows.append(sel[9][k1 - 8:k1 - 7, :])

        r1 = rank[0]
        n1 = jnp.zeros((nk, tn), F32)
        for k1 in range(PEER_TOPK):
            n1 = jnp.where(r1 == float(k1), jnp.broadcast_to(n_rows[k1], (nk, tn)), n1)

        a_ref[h] = jnp.exp(sorig[0] - top1) * inv_z
        n1_ref[h] = n1
        packed = (pltpu.bitcast(jnp.exp(sorig[1] - top2).astype(BF16), jnp.uint32),
                  pltpu.bitcast(rank[1].astype(BF16), jnp.uint32))
        for which in range(2):
            for lb in range(tn // LANES):
                for rb in range(nk // BF16_ROWS):
                    br_ref[lb, rb, h, which] = packed[which][
                        rb * SUBLANES:(rb + 1) * SUBLANES, lb * LANES:(lb + 1) * LANES]
        return carry

    lax.fori_loop(0, PEER_HEADS, head_body, 0)


def _peer_route(h1t, wqt, keys, tn):
    d, t = h1t.shape
    nk = PEER_NKEYS
    route_spec = pl.BlockSpec((PEER_HEADS, nk, tn), lambda i: (0, 0, i))
    route_f32 = jax.ShapeDtypeStruct((PEER_HEADS, nk, t), F32)
    tile_dims = (nk // BF16_ROWS, PEER_HEADS, 2, SUBLANES, LANES)
    tile_spec = pl.BlockSpec((tn // LANES,) + tile_dims, lambda i: (i, 0, 0, 0, 0, 0))
    route_b16 = jax.ShapeDtypeStruct((t // LANES,) + tile_dims, jnp.uint32)
    return pl.pallas_call(
        _peer_route_kernel,
        grid=(t // tn,),
        in_specs=[
            pl.BlockSpec((d, tn), lambda i: (0, i)),
            pl.BlockSpec(wqt.shape, lambda i: (0, 0)),
            pl.BlockSpec(keys.shape, lambda i: (0, 0, 0)),
        ],
        out_specs=[route_spec, route_spec, tile_spec],
        out_shape=[route_f32, route_f32, route_b16],
        scratch_shapes=[
            pltpu.VMEM((wqt.shape[0], tn), BF16),
            pltpu.VMEM((2, nk, tn), F32),
            pltpu.VMEM((2, nk, tn), F32),
            pltpu.VMEM((2, nk, tn), F32),
            pltpu.VMEM((2, PEER_TOPK, tn), F32),
        ],
        compiler_params=pltpu.CompilerParams(
            dimension_semantics=("parallel",), vmem_limit_bytes=VMEM_LIMIT),
        name="peer_route",
    )(h1t, wqt, keys)


def _peer_gate_act(row, j, lane_blocks, st_ref, at_ref, a_ref, n1_ref, br_ref):
    rows = BF16_ROWS
    a_full = [a_ref[h, row:row + 1, :] for h in range(PEER_HEADS)]
    n_full = [n1_ref[h, row:row + 1, :] for h in range(PEER_HEADS)]
    for lb in lane_blocks:
        ls = slice(lb * LANES, (lb + 1) * LANES)
        arow = [jnp.broadcast_to(a_full[h][:, ls], (rows, LANES)).astype(BF16)
                for h in range(PEER_HEADS)]
        nrow = [jnp.broadcast_to(n_full[h][:, ls], (rows, LANES)).astype(BF16)
                for h in range(PEER_HEADS)]
        for rb in range(PEER_NKEYS // rows):
            w = None
            for h in range(PEER_HEADS):
                bb = pltpu.bitcast(br_ref[lb, rb, h, 0], BF16)
                r2 = pltpu.bitcast(br_ref[lb, rb, h, 1], BF16)
                term = arow[h] * jnp.where(r2 < nrow[h], bb, jnp.zeros_like(bb))
                w = term if w is None else w + term
            c0 = j * PEER_NKEYS + rb * rows
            at_ref[c0:c0 + rows, ls] = w * _gelu_exact(st_ref[c0:c0 + rows, ls]).astype(BF16)


def _peer_dense_kernel(xt_ref, u0_ref, ub_ref, un_ref, va_ref, vb_ref,
                       a_ref, n1_ref, br_ref, acc_ref, st_a, st_b, at_a, at_b):
    e = pl.program_id(1)
    tm = xt_ref.shape[1]
    n_lb = tm // LANES
    halves = (tuple(range(n_lb // 2)), tuple(range(n_lb // 2, n_lb)))

    def scores(u_ref, st_ref):
        st_ref[...] = jnp.dot(u_ref[...], xt_ref[...], preferred_element_type=F32)

    def values(v_ref, at_ref, lbs):
        hs = slice(lbs[0] * LANES, (lbs[-1] + 1) * LANES)
        acc_ref[:, hs] += jnp.dot(v_ref[...], at_ref[:, hs], preferred_element_type=F32)

    def gate(parity, lbs, st_ref, at_ref):
        for j in range(PEER_SUBS):
            _peer_gate_act(parity * PEER_SUBS + j, j, lbs, st_ref, at_ref, a_ref, n1_ref, br_ref)

    @pl.when(e == 0)
    def _():
        acc_ref[...] = jnp.zeros_like(acc_ref)
        scores(u0_ref, st_a)

    scores(ub_ref, st_b)
    for lbs in halves:
        gate(0, lbs, st_a, at_a)
        values(va_ref, at_a, lbs)
    scores(un_ref, st_a)
    for lbs in halves:
        gate(1, lbs, st_b, at_b)
        values(vb_ref, at_b, lbs)


def _peer_dense(h1t, u_b, vt_b, routes, tm):
    d, t = h1t.shape
    n_exp = u_b.shape[0]
    ch = PEER_CHUNK
    n_ch = n_exp // ch
    assert n_exp % (2 * ch) == 0
    a_t, n1_t, br_t = routes
    once = pl.Buffered(1)
    route_spec = pl.BlockSpec((PEER_HEADS, 2 * PEER_SUBS, tm), lambda i, e: (0, e, i))
    tile_spec = pl.BlockSpec((tm // LANES,) + br_t.shape[1:], lambda i, e: (i, 0, 0, 0, 0, 0),
                             pipeline_mode=once)
    vec = pl.BlockSpec((1, d), lambda i, e: (0, 0))
    return pl.pallas_call(
        _peer_dense_kernel,
        grid=(t // tm, n_ch // 2),
        in_specs=[
            pl.BlockSpec((d, tm), lambda i, e: (0, i), pipeline_mode=once),
            pl.BlockSpec((ch, d), lambda i, e: (0, 0), pipeline_mode=once),
            pl.BlockSpec((ch, d), lambda i, e: (2 * e + 1, 0)),
            pl.BlockSpec((ch, d), lambda i, e: (jnp.minimum(2 * e + 2, n_ch - 1), 0)),
            pl.BlockSpec((d, ch), lambda i, e: (0, 2 * e)),
            pl.BlockSpec((d, ch), lambda i, e: (0, 2 * e + 1)),
            route_spec, route_spec, tile_spec,
        ],
        out_specs=pl.BlockSpec((d, tm), lambda i, e: (0, i)),
        out_shape=jax.ShapeDtypeStruct((d, t), F32),
        scratch_shapes=[
            pltpu.VMEM((ch, tm), F32),
            pltpu.VMEM((ch, tm), F32),
            pltpu.VMEM((ch, tm), BF16),
            pltpu.VMEM((ch, tm), BF16),
        ],
        compiler_params=pltpu.CompilerParams(
            dimension_semantics=("parallel", "arbitrary"), vmem_limit_bytes=VMEM_LIMIT),
        name="peer_dense",
    )(h1t, u_b, u_b, u_b, vt_b, vt_b, a_t, n1_t, br_t)


def _peer_out_kernel(pt_ref, h1_ref, g_ref, b_ref, o_ref):
    r = ALPHA * h1_ref[...] + pt_ref[...].T
    o_ref[...] = _layer_norm(r, g_ref[...], b_ref[...])


def _peer_out(peer_t, h1, g2, b2, tm):
    t, d = h1.shape
    vec = pl.BlockSpec((1, d), lambda i: (0, 0))
    return pl.pallas_call(
        _peer_out_kernel,
        grid=(t // tm,),
        in_specs=[pl.BlockSpec((d, tm), lambda i: (0, i)),
                  pl.BlockSpec((tm, d), lambda i: (i, 0)), vec, vec],
        out_specs=pl.BlockSpec((tm, d), lambda i: (i, 0)),
        out_shape=jax.ShapeDtypeStruct((t, d), F32),
        compiler_params=pltpu.CompilerParams(
            dimension_semantics=("parallel",), vmem_limit_bytes=VMEM_LIMIT),
        name="peer_out",
    )(peer_t, h1, g2, b2)


def _pick(n, prefs):
    for p in prefs:
        if n % p == 0:
            return p
    return n


def kernel(x, meta_tokens, ln_in_g, ln_in_b, w_in, pool_w, pool_scale, conv_w, conv_b, dt_bias,
           A_log, D_skip, ssd_norm_g, w_out, ln1_g, ln1_b, peer_wq, peer_keys, peer_u, peer_v,
           ln2_g, ln2_b):
    bsz, seq, d = x.shape
    assert seq % CHUNK == 0 and w_in.shape[0] == DEPTH
    t = bsz * seq
    n_chunks = seq // CHUNK

    w0 = w_in[0]
    o1, o2, o3 = POOL_WIDTH, POOL_WIDTH + SSD_WIDTH, POOL_WIDTH + SSD_WIDTH + CONV_DIM
    w_main = jnp.concatenate(
        [w0[:, o1:o2], w0[:, o2:o2 + SSD_WIDTH], w0[:, :o1], w0[:, o2 + SSD_WIDTH:o3]],
        axis=1).astype(BF16)
    w_dt = jnp.pad(w0[:, o3:], ((0, 0), (0, LANES - SSD_HEADS))).astype(BF16)
    g_in = ln_in_g.reshape(1, d)
    b_in = ln_in_b.reshape(1, d)
    pad_h = (0, LANES - SSD_HEADS)
    head_of_col = np.arange(SSD_WIDTH) // SSD_HEAD_DIM
    hexp = jnp.asarray((np.arange(LANES)[:, None] == head_of_col[None, :]).astype(np.float32))
    mixer_params = (
        conv_w[0], conv_b[0].reshape(1, CONV_DIM),
        jnp.pad(dt_bias[0], pad_h).reshape(1, LANES), jnp.pad(A_log[0], pad_h).reshape(1, LANES),
        jnp.repeat(D_skip[0], SSD_HEAD_DIM).reshape(1, SSD_WIDTH),
        ssd_norm_g[0].reshape(1, SSD_WIDTH),
        pool_w[0].astype(BF16), pool_scale[0].reshape(1, POOL_WIDTH), hexp)
    w_out_b = w_out[0].astype(BF16)
    wqt = peer_wq[0].T.astype(BF16)
    keys_b = peer_keys[0].astype(BF16)
    u_b = peer_u[0].astype(BF16)
    vt_b = peer_v[0].T.astype(BF16)

    proj_m, dt_m = _ln_inproj(meta_tokens, g_in, b_in, w_main, w_dt, N_META, _pick(PROJ_MAIN, (1024,)))
    n_pad = CHUNK - N_META
    proj_m = jnp.pad(proj_m, ((n_pad, 0), (0, 0)))
    dt_m = jnp.pad(dt_m, ((n_pad, 0), (0, 0)), constant_values=-1e30)
    zero_carries = (jnp.zeros((SSD_GROUPS, SSD_STATE, GROUP_WIDTH), F32),
                    jnp.zeros((CONV_TAIL, CONV_DIM), F32),
                    jnp.zeros((POOL_TAIL, POOL_WIDTH), F32))
    _, st_m, ct_m, pt_m = _mixer(proj_m, dt_m, zero_carries, mixer_params, 1, 1)
    carries = (st_m[0], ct_m[0], pt_m[0])

    x2d = x.reshape(t, d)
    tm1 = _pick(t, (1024, 512, 256, 128))
    proj, dt_raw = _ln_inproj(x2d, g_in, b_in, w_main, w_dt, tm1, _pick(PROJ_MAIN, (1024,)))
    mix, _, _, _ = _mixer(proj, dt_raw, carries, mixer_params, bsz, n_chunks)
    tm3 = _pick(t, (512, 256, 128))
    h1, h1t = _outproj_ln(mix, w_out_b, x2d, g_in, b_in, ln1_g[0].reshape(1, d),
                          ln1_b[0].reshape(1, d), tm3)
    tn4 = _pick(t, (512, 256, 128))
    routes = _peer_route(h1t, wqt, keys_b, tn4)
    tm5 = _pick(t, (1024, 512, 256, 128))
    peer_t = _peer_dense(h1t, u_b, vt_b, routes, tm5)
    out = _peer_out(peer_t, h1, ln2_g[0].reshape(1, d), ln2_b[0].reshape(1, d),
                    _pick(t, (512, 256, 128)))
    return out.reshape(bsz, seq, d)
```

```python
import functools
import math

import numpy as np
import jax
import jax.numpy as jnp
from jax import lax
from jax.experimental import pallas as pl
from jax.experimental.pallas import tpu as pltpu

F32 = jnp.float32
BF16 = jnp.bfloat16

N_META = 16
LN_EPS = 1e-5
DEPTH = 1
ALPHA = (2 * DEPTH) ** 0.25
POOL_WINDOWS = (2, 4, 8, 16)
POOL_GROUP_DIM = 256
POOL_WIDTH = 1024
SSD_HEAD_DIM = 64
SSD_WIDTH = 3072
SSD_HEADS = 48
SSD_GROUPS = 8
SSD_STATE = 128
HEADS_PER_GROUP = SSD_HEADS // SSD_GROUPS
GROUP_WIDTH = HEADS_PER_GROUP * SSD_HEAD_DIM
CONV_K = 4
CONV_DIM = SSD_WIDTH + 2 * SSD_GROUPS * SSD_STATE
MIX_WIDTH = POOL_WIDTH + SSD_WIDTH
PROJ_MAIN = POOL_WIDTH + SSD_WIDTH + CONV_DIM
PEER_HEADS = 8
PEER_NKEYS = 128
PEER_TOPK = 16
PEER_HALF = 128
PEER_CHUNK = 512
PEER_SUBS = PEER_CHUNK // PEER_NKEYS
PEER_PIECE = 512

LANES = 128
SUBLANES = 8
BF16_ROWS = 2 * SUBLANES
CHUNK = 128
CONV_TAIL = SUBLANES
POOL_TAIL = 16
VMEM_LIMIT = 60 * 1024 * 1024


def _layer_norm(x, g, b):
    mu = jnp.mean(x, axis=-1, keepdims=True)
    xc = x - mu
    var = jnp.mean(xc * xc, axis=-1, keepdims=True)
    return xc * lax.rsqrt(var + LN_EPS) * g + b


def _silu(x):
    return (0.5 * x) * (1.0 + jnp.tanh(0.5 * x))


def _softplus(x):
    return jnp.maximum(x, 0.0) + jnp.log1p(jnp.exp(-jnp.abs(x)))


def _gelu_exact(x):
    sqrt_half = np.float32(np.sqrt(0.5))
    return 0.5 * x * (1.0 + lax.erf(x * sqrt_half))


def _ln_inproj_kernel(x_ref, g_ref, b_ref, w_ref, wdt_ref, o_ref, dt_ref, hb_ref):
    @pl.when(pl.program_id(1) == 0)
    def _():
        h = _layer_norm(x_ref[...], g_ref[...], b_ref[...]).astype(BF16)
        hb_ref[...] = h
        dt_ref[...] = jnp.dot(h, wdt_ref[...], preferred_element_type=F32)

    o_ref[...] = jnp.dot(hb_ref[...], w_ref[...], preferred_element_type=F32)


def _ln_inproj(x2d, g, b, w_main, w_dt, tm, tn):
    t, d = x2d.shape
    n = w_main.shape[1]
    return pl.pallas_call(
        _ln_inproj_kernel,
        grid=(t // tm, n // tn),
        in_specs=[
            pl.BlockSpec((tm, d), lambda i, j: (i, 0)),
            pl.BlockSpec((1, d), lambda i, j: (0, 0)),
            pl.BlockSpec((1, d), lambda i, j: (0, 0)),
            pl.BlockSpec((d, tn), lambda i, j: (0, j)),
            pl.BlockSpec((d, LANES), lambda i, j: (0, 0)),
        ],
        out_specs=[
            pl.BlockSpec((tm, tn), lambda i, j: (i, j)),
            pl.BlockSpec((tm, LANES), lambda i, j: (i, 0)),
        ],
        out_shape=[
            jax.ShapeDtypeStruct((t, n), F32),
            jax.ShapeDtypeStruct((t, LANES), F32),
        ],
        scratch_shapes=[pltpu.VMEM((tm, d), BF16)],
        compiler_params=pltpu.CompilerParams(
            dimension_semantics=("parallel", "arbitrary"), vmem_limit_bytes=VMEM_LIMIT),
        name="ln_inproj",
    )(x2d, g, b, w_main, w_dt)


def _mixer_kernel(z_ref, x_ref, p_ref, bm_ref, cm_ref, dt_ref,
                  state0_ref, ctail0_ref, ptail0_ref,
                  convw_ref, convb_ref, dtb_ref, alog_ref, dskip_ref, normg_ref,
                  poolw_ref, pscale_ref,
                  mix_ref, state_out_ref, ctail_out_ref, ptail_out_ref,
                  cbuf, act, ybuf, pbuf, state):
    q = CHUNK
    c = pl.program_id(1)

    @pl.when(c == 0)
    def _init():
        state[...] = state0_ref[...]
        cbuf[0:CONV_TAIL, :] = ctail0_ref[...]
        pbuf[0:POOL_TAIL, :] = ptail0_ref[...]

    cbuf[CONV_TAIL:CONV_TAIL + q, 0:SSD_WIDTH] = x_ref[...]
    cbuf[CONV_TAIL:CONV_TAIL + q, SSD_WIDTH:SSD_WIDTH + 1024] = bm_ref[...]
    cbuf[CONV_TAIL:CONV_TAIL + q, SSD_WIDTH + 1024:CONV_DIM] = cm_ref[...]
    strip = 512
    for s in range(0, CONV_DIM, strip):
        ext = cbuf[:, s:s + strip]
        acc = convb_ref[:, s:s + strip] + convw_ref[CONV_K - 1:CONV_K, s:s + strip] * ext[CONV_TAIL:, :]
        for back in range(1, CONV_K):
            shifted = pltpu.roll(ext, back, axis=0)[CONV_TAIL:, :]
            acc = acc + convw_ref[CONV_K - 1 - back:CONV_K - back, s:s + strip] * shifted
        act[:, s:s + strip] = _silu(acc)
    cbuf[0:CONV_TAIL, :] = cbuf[q:q + CONV_TAIL, :]

    dt = _softplus(dt_ref[...] + dtb_ref[...])
    a_neg = -jnp.exp(alog_ref[...])
    adt = dt * a_neg
    row = lax.broadcasted_iota(jnp.int32, (q, q), 0)
    col = lax.broadcasted_iota(jnp.int32, (q, q), 1)
    causal = row >= col
    tril = jnp.where(causal, 1.0, 0.0).astype(F32)
    acs = jnp.dot(tril, adt, preferred_element_type=F32, precision=lax.Precision.HIGHEST)
    aend = acs[q - 1:q, :]
    wmat = jnp.exp(aend - acs) * dt
    acs_t = acs.T
    dt_t = dt.T
    exp_end = jnp.exp(aend)
    lane = lax.broadcasted_iota(jnp.int32, (q, LANES), 1)
    low_half = lane < SSD_HEAD_DIM
    low_half_row = low_half[0:1, :]

    for g in range(SSD_GROUPS):
        bg = act[:, SSD_WIDTH + g * SSD_STATE:SSD_WIDTH + (g + 1) * SSD_STATE]
        cg = act[:, SSD_WIDTH + 1024 + g * SSD_STATE:SSD_WIDTH + 1024 + (g + 1) * SSD_STATE]
        bgb = bg.astype(BF16)
        cb = lax.dot_general(cg.astype(BF16), bgb, (((1,), (1,)), ((), ())),
                             preferred_element_type=F32)
        st_g = state[g]
        st_b = st_g.astype(BF16)
        xw_parts = []
        dec_parts = []
        for pr in range(HEADS_PER_GROUP // 2):
            h0 = g * HEADS_PER_GROUP + 2 * pr
            c0 = h0 * SSD_HEAD_DIM
            xp = act[:, c0:c0 + LANES]
            xpb = xp.astype(BF16)
            stp = st_b[:, pr * LANES:(pr + 1) * LANES]
            ys = []
            for h in (h0, h0 + 1):
                colb = jnp.broadcast_to(acs[:, h:h + 1], (q, q))
                rowb = jnp.broadcast_to(acs_t[h:h + 1, :], (q, q))
                lmat = jnp.exp(jnp.where(causal, colb - rowb, -jnp.inf))
                m = (cb * lmat * jnp.broadcast_to(dt_t[h:h + 1, :], (q, q))).astype(BF16)
                ce = (cg * jnp.exp(colb)).astype(BF16)
                ys.append(jnp.dot(m, xpb, preferred_element_type=F32)
                          + jnp.dot(ce, stp, preferred_element_type=F32))
            ypair = jnp.where(low_half, ys[0], ys[1]) + dskip_ref[:, c0:c0 + LANES] * xp
            ybuf[:, c0:c0 + LANES] = ypair
            wpair = jnp.where(low_half,
                              jnp.broadcast_to(wmat[:, h0:h0 + 1], (q, LANES)),
                              jnp.broadcast_to(wmat[:, h0 + 1:h0 + 2], (q, LANES)))
            xw_parts.append((xp * wpair).astype(BF16))
            dec_parts.append(jnp.where(low_half_row,
                                       jnp.broadcast_to(exp_end[:, h0:h0 + 1], (1, LANES)),
                                       jnp.broadcast_to(exp_end[:, h0 + 1:h0 + 2], (1, LANES))))
        xw = jnp.concatenate(xw_parts, axis=1)
        upd = lax.dot_general(bgb, xw, (((0,), (0,)), ((), ())),
                              preferred_element_type=F32)
        state[g] = st_g * jnp.concatenate(dec_parts, axis=1) + upd

    for g in range(SSD_GROUPS):
        cs = slice(g * GROUP_WIDTH, (g + 1) * GROUP_WIDTH)
        yg = ybuf[:, cs] * _silu(z_ref[:, cs])
        ms = jnp.mean(yg * yg, axis=-1, keepdims=True)
        yn = yg * lax.rsqrt(ms + LN_EPS) * normg_ref[:, cs]
        mix_ref[:, POOL_WIDTH + g * GROUP_WIDTH:POOL_WIDTH + (g + 1) * GROUP_WIDTH] = yn.astype(BF16)

    pbuf[POOL_TAIL:POOL_TAIL + q, :] = p_ref[...]
    for gi, w in enumerate(POOL_WINDOWS):
        cs = slice(gi * POOL_GROUP_DIM, (gi + 1) * POOL_GROUP_DIM)
        ext = pbuf[:, cs]
        cur = ext[POOL_TAIL:, :]
        acc = cur
        for j in range(1, w):
            acc = acc + pltpu.roll(ext, j, axis=0)[POOL_TAIL:, :]
        pooled = acc * (1.0 / w) - cur
        mixed = jnp.dot(pooled.astype(BF16), poolw_ref[gi], preferred_element_type=F32)
        mix_ref[:, cs] = (mixed * pscale_ref[:, cs]).astype(BF16)
    pbuf[0:POOL_TAIL, :] = pbuf[q:q + POOL_TAIL, :]

    @pl.when(c == pl.num_programs(1) - 1)
    def _fin():
        state_out_ref[0] = state[...]
        ctail_out_ref[0] = cbuf[0:CONV_TAIL, :]
        ptail_out_ref[0] = pbuf[0:POOL_TAIL, :]


def _mixer(proj, dt_raw, carries, params, bsz, n_chunks):
    state0, ctail0, ptail0 = carries
    q = CHUNK
    t = proj.shape[0]

    def tok(col_block):
        return lambda b, c: (b * n_chunks + c, col_block)

    def const(shape):
        nd = len(shape)
        return pl.BlockSpec(shape, lambda b, c: (0,) * nd)

    (convw, convb, dtb, alog, dskip, normg, poolw, pscale) = params
    return pl.pallas_call(
        _mixer_kernel,
        grid=(bsz, n_chunks),
        in_specs=[
            pl.BlockSpec((q, SSD_WIDTH), tok(0)),
            pl.BlockSpec((q, SSD_WIDTH), tok(1)),
            pl.BlockSpec((q, 1024), tok(6)),
            pl.BlockSpec((q, 1024), tok(7)),
            pl.BlockSpec((q, 1024), tok(8)),
            pl.BlockSpec((q, LANES), tok(0)),
            const(state0.shape), const(ctail0.shape), const(ptail0.shape),
            const(convw.shape), const(convb.shape), const(dtb.shape), const(alog.shape),
            const(dskip.shape), const(normg.shape), const(poolw.shape), const(pscale.shape),
        ],
        out_specs=[
            pl.BlockSpec((q, MIX_WIDTH), tok(0)),
            pl.BlockSpec((1,) + state0.shape, lambda b, c: (b, 0, 0, 0)),
            pl.BlockSpec((1,) + ctail0.shape, lambda b, c: (b, 0, 0)),
            pl.BlockSpec((1,) + ptail0.shape, lambda b, c: (b, 0, 0)),
        ],
        out_shape=[
            jax.ShapeDtypeStruct((t, MIX_WIDTH), BF16),
            jax.ShapeDtypeStruct((bsz,) + state0.shape, F32),
            jax.ShapeDtypeStruct((bsz,) + ctail0.shape, F32),
            jax.ShapeDtypeStruct((bsz,) + ptail0.shape, F32),
        ],
        scratch_shapes=[
            pltpu.VMEM((q + CONV_TAIL, CONV_DIM), F32),
            pltpu.VMEM((q, CONV_DIM), F32),
            pltpu.VMEM((q, SSD_WIDTH), F32),
            pltpu.VMEM((q + POOL_TAIL, POOL_WIDTH), F32),
            pltpu.VMEM(state0.shape, F32),
        ],
        compiler_params=pltpu.CompilerParams(
            dimension_semantics=("parallel", "arbitrary"), vmem_limit_bytes=VMEM_LIMIT),
        name="mixer",
    )(proj, proj, proj, proj, proj, dt_raw, state0, ctail0, ptail0,
      convw, convb, dtb, alog, dskip, normg, poolw, pscale)


def _outproj_ln_kernel(mix_ref, w_ref, x_ref, gin_ref, bin_ref, g1_ref, b1_ref,
                       h1_ref, h1t_ref):
    mixed = jnp.dot(mix_ref[...], w_ref[...], preferred_element_type=F32)
    h0 = _layer_norm(x_ref[...], gin_ref[...], bin_ref[...])
    h1 = _layer_norm(ALPHA * h0 + mixed, g1_ref[...], b1_ref[...])
    h1_ref[...] = h1
    h1t_ref[...] = h1.T.astype(BF16)


def _outproj_ln(mix, w_out, x2d, gin, bin_, g1, b1, tm):
    t, d = x2d.shape
    kdim = mix.shape[1]
    vec = pl.BlockSpec((1, d), lambda i: (0, 0))
    return pl.pallas_call(
        _outproj_ln_kernel,
        grid=(t // tm,),
        in_specs=[
            pl.BlockSpec((tm, kdim), lambda i: (i, 0)),
            pl.BlockSpec((kdim, d), lambda i: (0, 0), pipeline_mode=pl.Buffered(1)),
            pl.BlockSpec((tm, d), lambda i: (i, 0)),
            vec, vec, vec, vec,
        ],
        out_specs=[
            pl.BlockSpec((tm, d), lambda i: (i, 0)),
            pl.BlockSpec((d, tm), lambda i: (0, i)),
        ],
        out_shape=[
            jax.ShapeDtypeStruct((t, d), F32),
            jax.ShapeDtypeStruct((d, t), BF16),
        ],
        compiler_params=pltpu.CompilerParams(
            dimension_semantics=("parallel",), vmem_limit_bytes=VMEM_LIMIT),
        name="outproj_ln",
    )(mix, w_out, x2d, gin, bin_, g1, b1)


_CAND_GROUPS = 10


def _cand_valid_rows(gidx):
    if gidx <= 1:
        return 8
    if gidx == 9:
        return 8
    k1 = gidx - 1
    return PEER_TOPK // (k1 + 1)


def _peer_route_kernel(h1t_ref, wqt_ref, keys_ref, a_ref, n1_ref, br_ref,
                       q_ref, sorig, swork, rank, stop):
    tn = h1t_ref.shape[1]
    nk = PEER_NKEYS
    q_ref[...] = jnp.dot(wqt_ref[...], h1t_ref[...], preferred_element_type=F32).astype(BF16)
    key_iota = lax.broadcasted_iota(jnp.int32, (nk, tn), 0).astype(F32)
    sub_iota = lax.broadcasted_iota(jnp.int32, (SUBLANES, tn), 0)

    def head_body(h, carry):
        for side in range(2):
            r0 = pl.multiple_of((h * 2 + side) * PEER_HALF, PEER_HALF)
            s = jnp.dot(keys_ref[side], q_ref[pl.ds(r0, PEER_HALF), :],
                        preferred_element_type=F32)
            sorig[side] = s

        def extract_distinct():
            for side in range(2):
                rank[side] = jnp.full((nk, tn), -1.0, F32)

            def round_body(k, prev):
                nxt = []
                for side in range(2):
                    s = sorig[side]
                    below = s < prev[side]
                    m = jnp.max(jnp.where(below, s, -jnp.inf), axis=0, keepdims=True)
                    rank[side] = rank[side] + jnp.where(below, 1.0, 0.0)
                    stop[side, pl.ds(k, 1), :] = m
                    nxt.append(m)
                return tuple(nxt)

            inf_row = jnp.full((1, tn), jnp.inf, F32)
            last = lax.fori_loop(0, PEER_TOPK, round_body, (inf_row, inf_row))
            for side in range(2):
                rank[side] = rank[side] + jnp.where(sorig[side] < last[side], 1.0, 0.0)

        def extract_tie_broken():
            for side in range(2):
                swork[side] = sorig[side]
                rank[side] = jnp.full((nk, tn), float(PEER_TOPK), F32)

            def round_body(k, c2):
                kf = lax.convert_element_type(k, F32)
                for side in range(2):
                    sw = swork[side]
                    m = jnp.max(sw, axis=0, keepdims=True)
                    first = jnp.min(jnp.where(sw == m, key_iota, float(nk)), axis=0, keepdims=True)
                    hit = key_iota == first
                    rank[side] = jnp.where(hit, kf, rank[side])
                    swork[side] = jnp.where(hit, -jnp.inf, sw)
                    stop[side, pl.ds(k, 1), :] = m
                return c2

            lax.fori_loop(0, PEER_TOPK, round_body, 0)

        extract_distinct()
        excess = jnp.zeros((1, tn), F32)
        for side in range(2):
            ranked = jnp.sum(jnp.where(rank[side] < float(PEER_TOPK), 1.0, 0.0), axis=0, keepdims=True)
            excess = excess + jnp.abs(ranked - float(PEER_TOPK))

        @pl.when(jnp.max(excess) > 0.0)
        def _():
            extract_tie_broken()

        st1 = stop[0]
        st2 = stop[1]
        top1 = st1[0:1, :]
        top2 = st2[0:1, :]
        cmax = top1 + top2

        def bc(rowv):
            return jnp.broadcast_to(rowv, (SUBLANES, tn))

        groups = [st2[0:8, :] + bc(top1), st2[8:16, :] + bc(top1)]
        for k1 in range(1, 8):
            groups.append(st2[0:8, :] + bc(st1[k1:k1 + 1, :]))
        groups.append(st1[8:16, :] + bc(top2))

        counts = [jnp.zeros((SUBLANES, tn), F32) for _ in range(_CAND_GROUPS)]
        for gp in range(_CAND_GROUPS):
            for rp in range(_cand_valid_rows(gp)):
                cp = bc(groups[gp][rp:rp + 1, :])
                for gq in range(_CAND_GROUPS):
                    if gq > gp:
                        beats = cp >= groups[gq]
                    elif gq < gp:
                        beats = cp > groups[gq]
                    else:
                        beats = jnp.logical_or(
                            cp > groups[gq],
                            jnp.logical_and(cp == groups[gq], sub_iota > rp))
                    counts[gq] = counts[gq] + jnp.where(beats, 1.0, 0.0)
        sel = []
        zsum = jnp.zeros((SUBLANES, tn), F32)
        for gq in range(_CAND_GROUPS):
            ok = jnp.logical_and(counts[gq] < float(PEER_TOPK), sub_iota < _cand_valid_rows(gq))
            sg = jnp.where(ok, 1.0, 0.0)
            sel.append(sg)
            zsum = zsum + sg * jnp.exp(groups[gq] - bc(cmax))
        inv_z = 1.0 / jnp.sum(zsum, axis=0, keepdims=True)

        n_rows = [jnp.sum(sel[0] + sel[1], axis=0, keepdims=True)]
        for k1 in range(1, 8):
            n_rows.append(jnp.sum(sel[k1 + 1], axis=0, keepdims=True))
        for k1 in range(8, 16):
            n_rows.append(sel[9][k1 - 8:k1 - 7, :])

        r1 = rank[0]
        n1 = jnp.zeros((nk, tn), F32)
        for k1 in range(PEER_TOPK):
            n1 = jnp.where(r1 == float(k1), jnp.broadcast_to(n_rows[k1], (nk, tn)), n1)

        a_ref[h] = jnp.exp(sorig[0] - top1) * inv_z
        n1_ref[h] = n1
        packed = (pltpu.bitcast(jnp.exp(sorig[1] - top2).astype(BF16), jnp.uint32),
                  pltpu.bitcast(rank[1].astype(BF16), jnp.uint32))
        for which in range(2):
            for lb in range(tn // LANES):
                for rb in range(nk // BF16_ROWS):
                    br_ref[lb, rb, h, which] = packed[which][
                        rb * SUBLANES:(rb + 1) * SUBLANES, lb * LANES:(lb + 1) * LANES]
        return carry

    lax.fori_loop(0, PEER_HEADS, head_body, 0)


def _peer_route(h1t, wqt, keys, tn):
    d, t = h1t.shape
    nk = PEER_NKEYS
    route_spec = pl.BlockSpec((PEER_HEADS, nk, tn), lambda i: (0, 0, i))
    route_f32 = jax.ShapeDtypeStruct((PEER_HEADS, nk, t), F32)
    tile_dims = (nk // BF16_ROWS, PEER_HEADS, 2, SUBLANES, LANES)
    tile_spec = pl.BlockSpec((tn // LANES,) + tile_dims, lambda i: (i, 0, 0, 0, 0, 0))
    route_b16 = jax.ShapeDtypeStruct((t // LANES,) + tile_dims, jnp.uint32)
    return pl.pallas_call(
        _peer_route_kernel,
        grid=(t // tn,),
        in_specs=[
            pl.BlockSpec((d, tn), lambda i: (0, i)),
            pl.BlockSpec(wqt.shape, lambda i: (0, 0)),
            pl.BlockSpec(keys.shape, lambda i: (0, 0, 0)),
        ],
        out_specs=[route_spec, route_spec, tile_spec],
        out_shape=[route_f32, route_f32, route_b16],
        scratch_shapes=[
            pltpu.VMEM((wqt.shape[0], tn), BF16),
            pltpu.VMEM((2, nk, tn), F32),
            pltpu.VMEM((2, nk, tn), F32),
            pltpu.VMEM((2, nk, tn), F32),
            pltpu.VMEM((2, PEER_TOPK, tn), F32),
        ],
        compiler_params=pltpu.CompilerParams(
            dimension_semantics=("parallel",), vmem_limit_bytes=VMEM_LIMIT),
        name="peer_route",
    )(h1t, wqt, keys)


def _peer_gate_act(row, j, lane_blocks, st_ref, at_ref, a_ref, n1_ref, br_ref):
    rows = BF16_ROWS
    a_full = [a_ref[h, row:row + 1, :] for h in range(PEER_HEADS)]
    n_full = [n1_ref[h, row:row + 1, :] for h in range(PEER_HEADS)]
    for lb in lane_blocks:
        ls = slice(lb * LANES, (lb + 1) * LANES)
        arow = [jnp.broadcast_to(a_full[h][:, ls], (rows, LANES)).astype(BF16)
                for h in range(PEER_HEADS)]
        nrow = [jnp.broadcast_to(n_full[h][:, ls], (rows, LANES)).astype(BF16)
                for h in range(PEER_HEADS)]
        for rb in range(PEER_NKEYS // rows):
            w = None
            for h in range(PEER_HEADS):
                bb = pltpu.bitcast(br_ref[lb, rb, h, 0], BF16)
                r2 = pltpu.bitcast(br_ref[lb, rb, h, 1], BF16)
                term = arow[h] * jnp.where(r2 < nrow[h], bb, jnp.zeros_like(bb))
                w = term if w is None else w + term
            c0 = j * PEER_NKEYS + rb * rows
            at_ref[c0:c0 + rows, ls] = w * _gelu_exact(st_ref[c0:c0 + rows, ls]).astype(BF16)


def _peer_dense_kernel(xt_ref, u0_ref, ub_ref, un_ref, va_ref, vb_ref,
                       a_ref, n1_ref, br_ref, acc_ref, st_a, st_b, at_a, at_b):
    e = pl.program_id(1)
    tm = xt_ref.shape[1]
    n_lb = tm // LANES
    n_piece = max(1, tm // PEER_PIECE)
    per = n_lb // n_piece
    halves = tuple(tuple(range(p * per, (p + 1) * per)) for p in range(n_piece))

    def scores(u_ref, st_ref):
        st_ref[...] = jnp.dot(u_ref[...], xt_ref[...], preferred_element_type=F32)

    def values(v_ref, at_ref, lbs):
        hs = slice(lbs[0] * LANES, (lbs[-1] + 1) * LANES)
        acc_ref[:, hs] += jnp.dot(v_ref[...], at_ref[:, hs], preferred_element_type=F32)

    def gate(parity, lbs, st_ref, at_ref):
        for j in range(PEER_SUBS):
            _peer_gate_act(parity * PEER_SUBS + j, j, lbs, st_ref, at_ref, a_ref, n1_ref, br_ref)

    @pl.when(e == 0)
    def _():
        acc_ref[...] = jnp.zeros_like(acc_ref)
        scores(u0_ref, st_a)

    scores(ub_ref, st_b)
    for lbs in halves:
        gate(0, lbs, st_a, at_a)
        values(va_ref, at_a, lbs)
    scores(un_ref, st_a)
    for lbs in halves:
        gate(1, lbs, st_b, at_b)
        values(vb_ref, at_b, lbs)


def _peer_dense(h1t, u_b, vt_b, routes, tm):
    d, t = h1t.shape
    n_exp = u_b.shape[0]
    ch = PEER_CHUNK
    n_ch = n_exp // ch
    assert n_exp % (2 * ch) == 0
    a_t, n1_t, br_t = routes
    once = pl.Buffered(1)
    route_spec = pl.BlockSpec((PEER_HEADS, 2 * PEER_SUBS, tm), lambda i, e: (0, e, i))
    tile_spec = pl.BlockSpec((tm // LANES,) + br_t.shape[1:], lambda i, e: (i, 0, 0, 0, 0, 0),
                             pipeline_mode=once)
    vec = pl.BlockSpec((1, d), lambda i, e: (0, 0))
    return pl.pallas_call(
        _peer_dense_kernel,
        grid=(t // tm, n_ch // 2),
        in_specs=[
            pl.BlockSpec((d, tm), lambda i, e: (0, i), pipeline_mode=once),
            pl.BlockSpec((ch, d), lambda i, e: (0, 0), pipeline_mode=once),
            pl.BlockSpec((ch, d), lambda i, e: (2 * e + 1, 0)),
            pl.BlockSpec((ch, d), lambda i, e: (jnp.minimum(2 * e + 2, n_ch - 1), 0)),
            pl.BlockSpec((d, ch), lambda i, e: (0, 2 * e)),
            pl.BlockSpec((d, ch), lambda i, e: (0, 2 * e + 1)),
            route_spec, route_spec, tile_spec,
        ],
        out_specs=pl.BlockSpec((d, tm), lambda i, e: (0, i)),
        out_shape=jax.ShapeDtypeStruct((d, t), F32),
        scratch_shapes=[
            pltpu.VMEM((ch, tm), F32),
            pltpu.VMEM((ch, tm), F32),
            pltpu.VMEM((ch, tm), BF16),
            pltpu.VMEM((ch, tm), BF16),
        ],
        compiler_params=pltpu.CompilerParams(
            dimension_semantics=("parallel", "arbitrary"), vmem_limit_bytes=VMEM_LIMIT),
        name="peer_dense",
    )(h1t, u_b, u_b, u_b, vt_b, vt_b, a_t, n1_t, br_t)


def _peer_out_kernel(pt_ref, h1_ref, g_ref, b_ref, o_ref):
    r = ALPHA * h1_ref[...] + pt_ref[...].T
    o_ref[...] = _layer_norm(r, g_ref[...], b_ref[...])


def _peer_out(peer_t, h1, g2, b2, tm):
    t, d = h1.shape
    vec = pl.BlockSpec((1, d), lambda i: (0, 0))
    return pl.pallas_call(
        _peer_out_kernel,
        grid=(t // tm,),
        in_specs=[pl.BlockSpec((d, tm), lambda i: (0, i)),
                  pl.BlockSpec((tm, d), lambda i: (i, 0)), vec, vec],
        out_specs=pl.BlockSpec((tm, d), lambda i: (i, 0)),
        out_shape=jax.ShapeDtypeStruct((t, d), F32),
        compiler_params=pltpu.CompilerParams(
            dimension_semantics=("parallel",), vmem_limit_bytes=VMEM_LIMIT),
        name="peer_out",
    )(peer_t, h1, g2, b2)


def _pick(n, prefs):
    for p in prefs:
        if n % p == 0:
            return p
    return n


def kernel(x, meta_tokens, ln_in_g, ln_in_b, w_in, pool_w, pool_scale, conv_w, conv_b, dt_bias,
           A_log, D_skip, ssd_norm_g, w_out, ln1_g, ln1_b, peer_wq, peer_keys, peer_u, peer_v,
           ln2_g, ln2_b):
    bsz, seq, d = x.shape
    assert seq % CHUNK == 0 and w_in.shape[0] == DEPTH
    t = bsz * seq
    n_chunks = seq // CHUNK

    w0 = w_in[0]
    o1, o2, o3 = POOL_WIDTH, POOL_WIDTH + SSD_WIDTH, POOL_WIDTH + SSD_WIDTH + CONV_DIM
    w_main = jnp.concatenate(
        [w0[:, o1:o2], w0[:, o2:o2 + SSD_WIDTH], w0[:, :o1], w0[:, o2 + SSD_WIDTH:o3]],
        axis=1).astype(BF16)
    w_dt = jnp.pad(w0[:, o3:], ((0, 0), (0, LANES - SSD_HEADS))).astype(BF16)
    g_in = ln_in_g.reshape(1, d)
    b_in = ln_in_b.reshape(1, d)
    pad_h = (0, LANES - SSD_HEADS)
    mixer_params = (
        conv_w[0], conv_b[0].reshape(1, CONV_DIM),
        jnp.pad(dt_bias[0], pad_h).reshape(1, LANES), jnp.pad(A_log[0], pad_h).reshape(1, LANES),
        jnp.repeat(D_skip[0], SSD_HEAD_DIM).reshape(1, SSD_WIDTH),
        ssd_norm_g[0].reshape(1, SSD_WIDTH),
        pool_w[0].astype(BF16), pool_scale[0].reshape(1, POOL_WIDTH))
    w_out_b = w_out[0].astype(BF16)
    wqt = peer_wq[0].T.astype(BF16)
    keys_b = peer_keys[0].astype(BF16)
    u_b = peer_u[0].astype(BF16)
    vt_b = peer_v[0].T.astype(BF16)

    proj_m, dt_m = _ln_inproj(meta_tokens, g_in, b_in, w_main, w_dt, N_META, _pick(PROJ_MAIN, (1024,)))
    n_pad = CHUNK - N_META
    proj_m = jnp.pad(proj_m, ((n_pad, 0), (0, 0)))
    dt_m = jnp.pad(dt_m, ((n_pad, 0), (0, 0)), constant_values=-1e30)
    zero_carries = (jnp.zeros((SSD_GROUPS, SSD_STATE, GROUP_WIDTH), F32),
                    jnp.zeros((CONV_TAIL, CONV_DIM), F32),
                    jnp.zeros((POOL_TAIL, POOL_WIDTH), F32))
    _, st_m, ct_m, pt_m = _mixer(proj_m, dt_m, zero_carries, mixer_params, 1, 1)
    carries = (st_m[0], ct_m[0], pt_m[0])

    x2d = x.reshape(t, d)
    tm1 = _pick(t, (1024, 512, 256, 128))
    proj, dt_raw = _ln_inproj(x2d, g_in, b_in, w_main, w_dt, tm1, _pick(PROJ_MAIN, (1024,)))
    mix, _, _, _ = _mixer(proj, dt_raw, carries, mixer_params, bsz, n_chunks)
    tm3 = _pick(t, (512, 256, 128))
    h1, h1t = _outproj_ln(mix, w_out_b, x2d, g_in, b_in, ln1_g[0].reshape(1, d),
                          ln1_b[0].reshape(1, d), tm3)
    tn4 = _pick(t, (512, 256, 128))
    routes = _peer_route(h1t, wqt, keys_b, tn4)
    tm5 = _pick(t, (1024, 512, 256, 128))
    peer_t = _peer_dense(h1t, u_b, vt_b, routes, tm5)
    out = _peer_out(peer_t, h1, ln2_g[0].reshape(1, d), ln2_b[0].reshape(1, d),
                    _pick(t, (512, 256, 128)))
    return out.reshape(bsz, seq, d)
```
